```python
import math
import jax, jax.numpy as jnp
from jax import lax
import numpy as np

D_MODEL = 2048
BATCH = 32
SEQ = 256
DEPTH = 2
DEC_BATCH = 8
DEC_SEQ = 4096
PAST_LEN = 512

GRID_W = 64
N_EVEN = (DEPTH + 1) // 2
N_ODD = DEPTH // 2
EPS = 1e-6
ATTN_QBLK = 128
ROPE_BASE = 10000.0
A_HEADS = 8
A_QK_DIM = 64
A_V_DIM = 128
A_WIDTH = A_HEADS * A_V_DIM
LRU_WIDTH = 1024
LRU_BLOCKS = 8
LRU_BLOCK = LRU_WIDTH // LRU_BLOCKS
CONV_W = 4
LRU_C = 8.0
EVEN_IN = 3 * A_WIDTH + 2 * LRU_WIDTH
EVEN_SPLITS = (A_WIDTH, 2 * A_WIDTH, 3 * A_WIDTH, 3 * A_WIDTH + LRU_WIDTH)
C_HEADS = 16
C_HEAD_DIM = 128
C_WIDTH = C_HEADS * C_HEAD_DIM
NA_KH = 8
NA_KW = 16
NA_QBLK = 16
NA_BAND = 32
PEER_HEADS = 8
PEER_NKEYS = 128
PEER_N = PEER_NKEYS * PEER_NKEYS
PEER_QDIM = 256
PEER_TOPK = 16
PEER_CHUNK = 128

kernel_name = 'peer_hybrid_diffusion_step'


def rmsnorm(x, g):
    xf = x.astype(jnp.float32)
    y = xf * lax.rsqrt(jnp.mean(xf * xf, axis=-1, keepdims=True) + EPS)
    return (y * g.astype(jnp.float32)).astype(x.dtype)


def modulation(cond, w_mod, b_mod):
    m = jax.nn.silu(cond) @ w_mod + b_mod
    return jnp.split(m[..., None, :], 6, axis=-1)


def pre(x, g, shift, scale):
    return rmsnorm(x, g) * (1.0 + scale) + shift


def axial_rope_tables(T):
    t = jnp.arange(T)
    row = (t // GRID_W).astype(jnp.float32)
    col = (t % GRID_W).astype(jnp.float32)
    n_freq = A_QK_DIM // 4
    inv = ROPE_BASE ** (-jnp.arange(n_freq, dtype=jnp.float32) / n_freq)
    ang = jnp.stack([row[:, None] * inv, col[:, None] * inv], axis=1)
    return jnp.cos(ang), jnp.sin(ang)


def apply_axial_rope(x, cos, sin):
    shp = x.shape
    xr = x.reshape(shp[:-1] + (2, 2, A_QK_DIM // 4))
    x1, x2 = xr[..., 0, :], xr[..., 1, :]
    c = cos[None, :, None, None].astype(x.dtype)
    s = sin[None, :, None, None].astype(x.dtype)
    out = jnp.stack([x1 * c - x2 * s, x2 * c + x1 * s], axis=-2)
    return out.reshape(shp)


def _query_blocks(q):
    B, T = q.shape[:2]
    nb = T // ATTN_QBLK
    return q.reshape((B, nb, ATTN_QBLK) + q.shape[2:]).swapaxes(0, 1)


def _merge_blocks(o):
    nb, B, qb = o.shape[:3]
    return o.swapaxes(0, 1).reshape((B, nb * qb) + o.shape[3:])


def diff_lambda(lam_p, lam_init):
    lp = lam_p.astype(jnp.float32)
    return jnp.exp(jnp.sum(lp[0] * lp[1])) - jnp.exp(jnp.sum(lp[2] * lp[3])) + lam_init


def diff_attention(q, k, v, lam):
    scale = A_QK_DIM ** -0.5

    def block(qi):
        s = jnp.einsum('bqhmd,bkhmd->bhmqk', qi, k).astype(jnp.float32) * scale
        p = jax.nn.softmax(s, axis=-1)
        w = p[:, :, 0] - lam * p[:, :, 1]
        return jnp.einsum('bhqk,bkhe->bqhe', w.astype(v.dtype), v)

    return _merge_blocks(lax.map(block, _query_blocks(q)))


def attention(q, k, v):
    scale = q.shape[-1] ** -0.5

    def block(qi):
        s = jnp.einsum('bqhd,bkhd->bhqk', qi, k).astype(jnp.float32) * scale
        p = jax.nn.softmax(s, axis=-1)
        return jnp.einsum('bhqk,bkhd->bqhd', p.astype(v.dtype), v)

    return _merge_blocks(lax.map(block, _query_blocks(q)))


def neighbourhood_attention(q, k, v, ck, cv, rpb):
    B, T, H, d = q.shape
    rows = T // GRID_W
    kh = min(NA_KH, rows)
    ncb = GRID_W // NA_QBLK
    qcol = np.arange(GRID_W).reshape(ncb, NA_QBLK)
    band0 = np.clip(np.arange(ncb) * NA_QBLK - NA_KW // 2, 0, GRID_W - NA_BAND)
    band_cols = band0[:, None] + np.arange(NA_BAND)[None, :]
    cs = np.clip(qcol - NA_KW // 2, 0, GRID_W - NA_KW)
    col_ok = (band_cols[:, None, :] >= cs[..., None]) & (band_cols[:, None, :] < cs[..., None] + NA_KW)
    col_off = np.clip(band_cols[:, None, :] - qcol[..., None] + NA_KW - 1, 0, 2 * NA_KW - 2)
    kg = k.reshape(B, rows, GRID_W, H, d)
    vg = v.reshape(B, rows, GRID_W, H, d)
    qg = q.reshape(B, rows, ncb, NA_QBLK, H, d)
    scale = d ** -0.5
    n_lat = kh * NA_BAND

    def row_fn(r):
        rs = jnp.clip(r - kh // 2, 0, rows - kh)
        qr = lax.dynamic_index_in_dim(qg, r, axis=1, keepdims=False)
        kb = lax.dynamic_slice_in_dim(kg, rs, kh, axis=1)[:, :, band_cols]
        vb = lax.dynamic_slice_in_dim(vg, rs, kh, axis=1)[:, :, band_cols]
        row_idx = rs + jnp.arange(kh) - r + (NA_KH - 1)
        bias = rpb[:, row_idx[None, None, :, None], col_off[:, :, None, :]]
        s_lat = jnp.einsum('bjqhd,bijkhd->bhjqik', qr, kb).astype(jnp.float32) * scale
        s_lat = s_lat + bias.astype(jnp.float32)[None]
        s_lat = jnp.where(col_ok[:, :, None, :], s_lat, -jnp.inf)
        s_ctx = jnp.einsum('bjqhd,bkhd->bhjqk', qr, ck).astype(jnp.float32) * scale
        s = jnp.concatenate([s_lat.reshape(B, H, ncb, NA_QBLK, n_lat), s_ctx], axis=-1)
        p = jax.nn.softmax(s, axis=-1).astype(v.dtype)
        p_lat = p[..., :n_lat].reshape(B, H, ncb, NA_QBLK, kh, NA_BAND)
        o = (jnp.einsum('bhjqik,bijkhd->bjqhd', p_lat, vb)
             + jnp.einsum('bhjqk,bkhd->bjqhd', p[..., n_lat:], cv))
        return o.reshape(B, GRID_W, H * d)

    o = lax.map(row_fn, jnp.arange(rows))
    return o.transpose(1, 0, 2, 3).reshape(B, T, H * d)


def centred_conv(x, w, b):
    T = x.shape[1]
    left = CONV_W // 2
    xp = jnp.pad(x, ((0, 0), (left, CONV_W - 1 - left), (0, 0)))
    y = xp[:, 0:T] * w[0]
    for j in range(1, CONV_W):
        y = y + xp[:, j:j + T] * w[j]
    return y + b


def block_diag(x, w, b):
    xb = x.reshape(x.shape[:-1] + (LRU_BLOCKS, LRU_BLOCK))
    return jnp.einsum('btni,nio->btno', xb, w).reshape(x.shape) + b


def rglru_direction(xc, h0, w_a, b_a, w_i, b_i, lam, reverse):
    r = jax.nn.sigmoid(block_diag(xc, w_a, b_a).astype(jnp.float32))
    i = jax.nn.sigmoid(block_diag(xc, w_i, b_i).astype(jnp.float32))
    log_a = -LRU_C * r * jax.nn.softplus(-lam.astype(jnp.float32))
    a = jnp.exp(log_a)
    bt = jnp.sqrt(-jnp.expm1(2.0 * log_a)) * i * xc.astype(jnp.float32)
    edge = -1 if reverse else 0
    bt = bt.at[:, edge].add(a[:, edge] * h0.astype(jnp.float32))

    def comb(e1, e2):
        a1, b1 = e1
        a2, b2 = e2
        return a1 * a2, a2 * b1 + b2

    _, h = lax.associative_scan(comb, (a, bt), reverse=reverse, axis=1)
    return h


def lru_mixer(xr, g, h0, conv_w, conv_b, w_a, b_a, w_i, b_i, lam):
    xc = centred_conv(xr, conv_w, conv_b)
    hf = rglru_direction(xc, h0[:, 0], w_a[0], b_a[0], w_i[0], b_i[0], lam[0], False)
    hb = rglru_direction(xc, h0[:, 1], w_a[1], b_a[1], w_i[1], b_i[1], lam[1], True)
    out = (hf + hb).astype(xr.dtype) * jax.nn.gelu(g)
    fin = jnp.stack([hf[:, -1], hb[:, 0]], axis=1).astype(xr.dtype)
    return out, fin


def even_mixer(xm, ctx, w_in, w_out, lam_p, subln_g, lam_init,
               conv_w, conv_b, w_a, b_a, w_i, b_i, lru_lam):
    B, T, _ = xm.shape
    q, k, v, xr, g = jnp.split(xm @ w_in, EVEN_SPLITS, axis=-1)
    q = q.reshape(B, T, A_HEADS, 2, A_QK_DIM)
    k = k.reshape(B, T, A_HEADS, 2, A_QK_DIM)
    v = v.reshape(B, T, A_HEADS, A_V_DIM)
    lam = diff_lambda(lam_p, lam_init)
    if ctx is None:
        kk, vv = k, v
        h0 = jnp.zeros((B, 2, LRU_WIDTH), xm.dtype)
    else:
        ck, cv, h0 = ctx
        cos, sin = axial_rope_tables(T)
        q = apply_axial_rope(q, cos, sin)
        k = apply_axial_rope(k, cos, sin)
        kk = jnp.concatenate([ck.reshape(B, -1, A_HEADS, 2, A_QK_DIM), k], axis=1)
        vv = jnp.concatenate([cv, v], axis=1)
    o_a = rmsnorm(diff_attention(q, kk, vv, lam), subln_g) * (1.0 - lam_init)
    o_b, fin = lru_mixer(xr, g, h0, conv_w, conv_b, w_a, b_a, w_i, b_i, lru_lam)
    out = jnp.concatenate([o_a.reshape(B, T, A_WIDTH), o_b], axis=-1) @ w_out
    return out, k.reshape(B, T, A_HEADS, 2 * A_QK_DIM), v, fin


def odd_mixer(xm, ctx, w_in, w_out, rpb):
    B, T, _ = xm.shape
    q, k, v = jnp.split(xm @ w_in, 3, axis=-1)
    q = q.reshape(B, T, C_HEADS, C_HEAD_DIM)
    k = k.reshape(B, T, C_HEADS, C_HEAD_DIM)
    v = v.reshape(B, T, C_HEADS, C_HEAD_DIM)
    if ctx is None:
        o = attention(q, k, v)
    else:
        o = neighbourhood_attention(q, k, v, ctx[0], ctx[1], rpb)
    return o.reshape(B, T, C_WIDTH) @ w_out, k, v


def peer(x, w_q, sub_keys, u_tab, v_tab):
    B, T, D = x.shape
    n = B * T
    xf = x.reshape(n, D)
    q = (xf @ w_q).reshape(n, PEER_HEADS, 2, PEER_QDIM // 2)
    s = jnp.einsum('nhpd,hpkd->nhpk', q, sub_keys).astype(jnp.float32)
    top_s, top_i = lax.top_k(s, PEER_TOPK)
    cand = top_s[:, :, 0, :, None] + top_s[:, :, 1, None, :]
    best_s, best_p = lax.top_k(cand.reshape(n, PEER_HEADS, PEER_TOPK * PEER_TOPK), PEER_TOPK)
    i1 = jnp.take_along_axis(top_i[:, :, 0], best_p // PEER_TOPK, axis=-1)
    i2 = jnp.take_along_axis(top_i[:, :, 1], best_p % PEER_TOPK, axis=-1)
    idx = i1 * PEER_NKEYS + i2
    gates = jax.nn.softmax(best_s, axis=-1).astype(x.dtype)
    nc = n // PEER_CHUNK

    def chunk(args):
        xc, ic, gc = args
        act = jax.nn.gelu(jnp.einsum('cd,chkd->chk', xc, u_tab[ic])) * gc
        return jnp.einsum('chk,chkd->cd', act, v_tab[ic])

    out = lax.map(chunk, (xf.reshape(nc, PEER_CHUNK, D),
                          idx.reshape(nc, PEER_CHUNK, PEER_HEADS, PEER_TOPK),
                          gates.reshape(nc, PEER_CHUNK, PEER_HEADS, PEER_TOPK)))
    return out.reshape(B, T, D)


def setup_inputs(seed: int = 0) -> dict:
    key = jax.random.key(seed)
    ks = iter(jax.random.split(key, 40))
    f32 = jnp.float32

    def nrm(shape, scale=1.0):
        return jax.random.normal(next(ks), shape, f32) * scale

    def gain(shape):
        return 1.0 + nrm(shape, 0.02)

    u = jax.random.uniform(next(ks), (N_EVEN, 2, LRU_WIDTH), f32, minval=0.9, maxval=0.999)
    a_base = u ** (1.0 / LRU_C)
    return {
        'x_prompt': nrm((BATCH, SEQ, D_MODEL)),
        'x_sample': nrm((DEC_BATCH, DEC_SEQ, D_MODEL)),
        'cache_a_k': nrm((DEC_BATCH, N_EVEN, PAST_LEN, A_HEADS, 2 * A_QK_DIM)),
        'cache_a_v': nrm((DEC_BATCH, N_EVEN, PAST_LEN, A_HEADS, A_V_DIM)),
        'state_lru': nrm((DEC_BATCH, N_EVEN, 2, LRU_WIDTH), 0.5),
        'cache_c_k': nrm((DEC_BATCH, N_ODD, PAST_LEN, C_HEADS, C_HEAD_DIM)),
        'cache_c_v': nrm((DEC_BATCH, N_ODD, PAST_LEN, C_HEADS, C_HEAD_DIM)),
        'c': nrm((DEC_BATCH, D_MODEL)),
        'c_ctx': nrm((D_MODEL,)),
        'w_mod': nrm((DEPTH, D_MODEL, 6 * D_MODEL), 0.5 * D_MODEL ** -0.5),
        'b_mod': nrm((DEPTH, 6 * D_MODEL), 0.02),
        'norm1_g': gain((DEPTH, D_MODEL)),
        'norm2_g': gain((DEPTH, D_MODEL)),
        'final_norm_g': gain((D_MODEL,)),
        'even_w_in': nrm((N_EVEN, D_MODEL, EVEN_IN), D_MODEL ** -0.5),
        'even_w_out': nrm((N_EVEN, A_WIDTH + LRU_WIDTH, D_MODEL), (A_WIDTH + LRU_WIDTH) ** -0.5),
        'a_lambda': nrm((N_EVEN, 4, A_QK_DIM), 0.1),
        'a_subln_g': gain((N_EVEN, A_V_DIM)),
        'lru_conv_w': nrm((N_EVEN, CONV_W, LRU_WIDTH), CONV_W ** -0.5),
        'lru_conv_b': nrm((N_EVEN, LRU_WIDTH), 0.02),
        'lru_w_a': nrm((N_EVEN, 2, LRU_BLOCKS, LRU_BLOCK, LRU_BLOCK), LRU_BLOCK ** -0.5),
        'lru_b_a': nrm((N_EVEN, 2, LRU_WIDTH), 0.02),
        'lru_w_i': nrm((N_EVEN, 2, LRU_BLOCKS, LRU_BLOCK, LRU_BLOCK), LRU_BLOCK ** -0.5),
        'lru_b_i': nrm((N_EVEN, 2, LRU_WIDTH), 0.02),
        'lru_lambda': jnp.log(a_base) - jnp.log1p(-a_base),
        'odd_w_in': nrm((N_ODD, D_MODEL, 3 * C_WIDTH), D_MODEL ** -0.5),
        'odd_w_out': nrm((N_ODD, C_WIDTH, D_MODEL), C_WIDTH ** -0.5),
        'na_rpb': nrm((N_ODD, C_HEADS, 2 * NA_KH - 1, 2 * NA_KW - 1), 0.1),
        'peer_w_q': nrm((DEPTH, D_MODEL, PEER_HEADS * PEER_QDIM), D_MODEL ** -0.5),
        'peer_sub_keys': nrm((DEPTH, PEER_HEADS, 2, PEER_NKEYS, PEER_QDIM // 2), (PEER_QDIM // 2) ** -0.5),
        'peer_u': nrm((DEPTH, PEER_N, D_MODEL), D_MODEL ** -0.5),
        'peer_v': nrm((DEPTH, PEER_N, D_MODEL), 0.05),
    }


def reference(x_prompt, x_sample, cache_a_k, cache_a_v, state_lru, cache_c_k, cache_c_v,
              c, c_ctx, w_mod, b_mod, norm1_g, norm2_g, final_norm_g,
              even_w_in, even_w_out, a_lambda, a_subln_g, lru_conv_w, lru_conv_b,
              lru_w_a, lru_b_a, lru_w_i, lru_b_i, lru_lambda,
              odd_w_in, odd_w_out, na_rpb, peer_w_q, peer_sub_keys, peer_u, peer_v):
    xp, xs = x_prompt, x_sample
    new_ak, new_av, new_lru, new_ck, new_cv = [], [], [], [], []
    for l in range(DEPTH):
        mc = modulation(c_ctx, w_mod[l], b_mod[l])
        ms = modulation(c, w_mod[l], b_mod[l])
        hp = pre(xp, norm1_g[l], mc[0], mc[1])
        hs = pre(xs, norm1_g[l], ms[0], ms[1])
        if l % 2 == 0:
            e = l // 2
            lam_init = 0.8 - 0.6 * math.exp(-0.3 * l)
            lp = (even_w_in[e], even_w_out[e], a_lambda[e], a_subln_g[e], lam_init,
                  lru_conv_w[e], lru_conv_b[e], lru_w_a[e], lru_b_a[e], lru_w_i[e],
                  lru_b_i[e], lru_lambda[e])
            op, kp, vp, sp = even_mixer(hp, None, *lp)
            os_, _, _, _ = even_mixer(hs, (cache_a_k[:, e], cache_a_v[:, e], state_lru[:, e]), *lp)
            new_ak.append(kp)
            new_av.append(vp)
            new_lru.append(sp)
        else:
            o = l // 2
            op, kp, vp = odd_mixer(hp, None, odd_w_in[o], odd_w_out[o], na_rpb[o])
            os_, _, _ = odd_mixer(hs, (cache_c_k[:, o], cache_c_v[:, o]),
                                  odd_w_in[o], odd_w_out[o], na_rpb[o])
            new_ck.append(kp)
            new_cv.append(vp)
        xp = xp + mc[2] * op
        xs = xs + ms[2] * os_
        hp = pre(xp, norm2_g[l], mc[3], mc[4])
        hs = pre(xs, norm2_g[l], ms[3], ms[4])
        xp = xp + mc[5] * peer(hp, peer_w_q[l], peer_sub_keys[l], peer_u[l], peer_v[l])
        xs = xs + ms[5] * peer(hs, peer_w_q[l], peer_sub_keys[l], peer_u[l], peer_v[l])
    y_prompt = rmsnorm(xp, final_norm_g)
    y_sample = rmsnorm(xs, final_norm_g)
    return (y_prompt, y_sample, jnp.stack(new_ak, axis=1), jnp.stack(new_av, axis=1),
            jnp.stack(new_lru, axis=1), jnp.stack(new_ck, axis=1), jnp.stack(new_cv, axis=1))
```

```python
import functools
import math

import numpy as np
import jax
import jax.numpy as jnp
from jax import lax
from jax.experimental import pallas as pl
from jax.experimental.pallas import tpu as pltpu

F32 = jnp.float32
BF16 = jnp.bfloat16

D_MODEL = 2048
GRID_W = 64
EPS = 1e-6
ROPE_BASE = 10000.0
A_HEADS = 8
A_QK_DIM = 64
A_V_DIM = 128
A_WIDTH = A_HEADS * A_V_DIM
LRU_WIDTH = 1024
LRU_BLOCKS = 8
LRU_BLOCK = LRU_WIDTH // LRU_BLOCKS
CONV_W = 4
LRU_C = 8.0
EVEN_IN = 3 * A_WIDTH + 2 * LRU_WIDTH
C_HEADS = 16
C_HEAD_DIM = 128
C_WIDTH = C_HEADS * C_HEAD_DIM
NA_KH = 8
NA_KW = 16
PEER_HEADS = 8
PEER_NKEYS = 128
PEER_N = PEER_NKEYS * PEER_NKEYS
PEER_QDIM = 256
PEER_TOPK = 16

LANES = 128
SUBLANES = 8
VMEM_LIMIT = 56 * 1024 * 1024

NT_DIMS = (((1,), (1,)), ((), ()))


def _cparams(sem):
    return pltpu.CompilerParams(dimension_semantics=sem, vmem_limit_bytes=VMEM_LIMIT)


def _gelu(x):
    c = math.sqrt(2.0 / math.pi)
    return 0.5 * x * (1.0 + jnp.tanh(c * (x + 0.044715 * (x * x * x))))


def _sigmoid(x):
    return 1.0 / (1.0 + jnp.exp(-x))


def _mod_kernel(c_ref, w_ref, b_ref, o_ref):
    c = c_ref[...]
    s = c * _sigmoid(c)
    o_ref[0] = jnp.dot(s.astype(BF16), w_ref[0].astype(BF16),
                       preferred_element_type=F32) + b_ref[0]


def modulation_all(cond, w_mod, b_mod):
    depth, d, n6 = w_mod.shape
    r = cond.shape[0]
    tn = 768
    return pl.pallas_call(
        _mod_kernel,
        grid=(depth, n6 // tn),
        in_specs=[pl.BlockSpec((r, d), lambda l, j: (0, 0)),
                  pl.BlockSpec((1, d, tn), lambda l, j: (l, 0, j)),
                  pl.BlockSpec((1, 1, tn), lambda l, j: (l, 0, j))],
        out_specs=pl.BlockSpec((1, r, tn), lambda l, j: (l, 0, j)),
        out_shape=jax.ShapeDtypeStruct((depth, r, n6), F32),
        compiler_params=_cparams(("arbitrary", "arbitrary")),
        name="modulation",
    )(cond, w_mod, b_mod.reshape(depth, 1, n6))


def _mod_row(i, tm, n_prompt, t_s):
    s = i * tm
    return jnp.where(s < n_prompt, 0, 1 + (s - n_prompt) // t_s)


def _norm_matmul_kernel(x_ref, g_ref, sh_ref, sc_ref, w_ref, o_ref, *rest, emit_h):
    if emit_h:
        h_out_ref, h_scr = rest
    else:
        (h_scr,) = rest

    @pl.when(pl.program_id(1) == 0)
    def _():
        x = x_ref[...]
        y = x * lax.rsqrt(jnp.mean(x * x, axis=-1, keepdims=True) + EPS)
        h = (y * g_ref[...]) * (1.0 + sc_ref[...]) + sh_ref[...]
        h_scr[...] = h.astype(BF16)
        if emit_h:
            h_out_ref[...] = h.astype(BF16)

    o_ref[...] = jnp.dot(h_scr[...], w_ref[...], preferred_element_type=F32).astype(o_ref.dtype)


def norm_matmul(x, g, mod, shift_idx, scale_idx, w, *, n_prompt, t_s, emit_h=False,
                tm=512, tn=512, out_dtype=F32):
    n, d = x.shape
    nout = w.shape[1]
    row = functools.partial(_mod_row, tm=tm, n_prompt=n_prompt, t_s=t_s)
    out_shape = [jax.ShapeDtypeStruct((n, nout), out_dtype)]
    out_specs = [pl.BlockSpec((tm, tn), lambda i, j: (i, j))]
    if emit_h:
        out_shape.append(jax.ShapeDtypeStruct((n, d), BF16))
        out_specs.append(pl.BlockSpec((tm, d), lambda i, j: (i, 0)))
    res = pl.pallas_call(
        functools.partial(_norm_matmul_kernel, emit_h=emit_h),
        grid=(n // tm, nout // tn),
        in_specs=[pl.BlockSpec((tm, d), lambda i, j: (i, 0)),
                  pl.BlockSpec((1, d), lambda i, j: (0, 0)),
                  pl.BlockSpec((None, None, 1, d), lambda i, j: (row(i), shift_idx, 0, 0)),
                  pl.BlockSpec((None, None, 1, d), lambda i, j: (row(i), scale_idx, 0, 0)),
                  pl.BlockSpec((d, tn), lambda i, j: (0, j))],
        out_specs=out_specs,
        out_shape=out_shape,
        scratch_shapes=[pltpu.VMEM((tm, d), BF16)],
        compiler_params=_cparams(("arbitrary", "arbitrary")),
        name="norm_matmul",
    )(x, g.reshape(1, d), mod, mod, w)
    return res if emit_h else res[0]


def _matmul_res_kernel(*refs, n_in):
    a_refs = refs[:n_in]
    w_refs = refs[n_in:2 * n_in]
    x_ref, gate_ref, o_ref = refs[2 * n_in:]
    acc = None
    for a_ref, w_ref in zip(a_refs, w_refs):
        p = jnp.dot(a_ref[...].astype(BF16), w_ref[...], preferred_element_type=F32)
        acc = p if acc is None else acc + p
    o_ref[...] = x_ref[...] + gate_ref[...] * acc


def matmul_res(a_list, w_list, x, mod, gate_idx, *, n_prompt, t_s, tm=512, tn=512):
    n, d = x.shape
    n_in = len(a_list)
    row = functools.partial(_mod_row, tm=tm, n_prompt=n_prompt, t_s=t_s)
    in_specs = ([pl.BlockSpec((tm, a.shape[1]), lambda i, j: (i, 0)) for a in a_list]
                + [pl.BlockSpec((w.shape[0], tn), lambda i, j: (0, j)) for w in w_list]
                + [pl.BlockSpec((tm, tn), lambda i, j: (i, j)),
                   pl.BlockSpec((None, None, 1, tn), lambda i, j: (row(i), gate_idx, 0, j))])
    return pl.pallas_call(
        functools.partial(_matmul_res_kernel, n_in=n_in),
        grid=(n // tm, d // tn),
        in_specs=in_specs,
        out_specs=pl.BlockSpec((tm, tn), lambda i, j: (i, j)),
        out_shape=jax.ShapeDtypeStruct((n, d), F32),
        compiler_params=_cparams(("arbitrary", "arbitrary")),
        name="matmul_res",
    )(*a_list, *w_list, x, mod)


def _rope(x, cos, sin_signed):
    lane = lax.broadcasted_iota(jnp.int32, x.shape, 1)
    first = (lane % 32) < 16
    partner = jnp.where(first, pltpu.roll(x, LANES - 16, axis=1), pltpu.roll(x, 16, axis=1))
    return x * cos + partner * sin_signed


def _diff_attn_kernel(*refs, has_ctx, lam_init, tq):
    if has_ctx:
        (q_ref, k_ref, v_ref, ck_ref, cv_ref, cos_ref, sin_ref, lam_ref, g_ref,
         o_ref, k_scr, v_scr) = refs
    else:
        q_ref, k_ref, v_ref, lam_ref, g_ref, o_ref, k_scr, v_scr = refs
    qi = pl.program_id(2)

    @pl.when(qi == 0)
    def _():
        k = k_ref[...]
        if has_ctx:
            k = _rope(k, cos_ref[...], sin_ref[...])
        k_scr[...] = k.astype(BF16)
        v_scr[...] = v_ref[...].astype(BF16)

    q = q_ref[...]
    if has_ctx:
        off = pl.multiple_of(qi * tq, tq)
        q = _rope(q, cos_ref[pl.ds(off, tq), :], sin_ref[pl.ds(off, tq), :])
    q = q * (A_QK_DIM ** -0.5)
    lane = lax.broadcasted_iota(jnp.int32, q.shape, 1)
    q2 = jnp.concatenate([jnp.where(lane < A_QK_DIM, q, 0.0),
                          jnp.where(lane >= A_QK_DIM, q, 0.0)], axis=0).astype(BF16)
    s = lax.dot_general(q2, k_scr[...], NT_DIMS, preferred_element_type=F32)
    mx = jnp.max(s, axis=-1, keepdims=True)
    if has_ctx:
        sc = lax.dot_general(q2, ck_ref[...].astype(BF16), NT_DIMS, preferred_element_type=F32)
        mx = jnp.maximum(mx, jnp.max(sc, axis=-1, keepdims=True))
    e = jnp.exp(s - mx)
    l = jnp.sum(e, axis=-1, keepdims=True)
    o = jnp.dot(e.astype(BF16), v_scr[...], preferred_element_type=F32)
    if has_ctx:
        ec = jnp.exp(sc - mx)
        l = l + jnp.sum(ec, axis=-1, keepdims=True)
        o = o + jnp.dot(ec.astype(BF16), cv_ref[...].astype(BF16), preferred_element_type=F32)
    o = o / l
    lp = lam_ref[...]
    lam = (jnp.exp(jnp.sum(lp[0:1] * lp[1:2], axis=-1, keepdims=True))
           - jnp.exp(jnp.sum(lp[2:3] * lp[3:4], axis=-1, keepdims=True)) + lam_init)
    d = o[:tq] - lam * o[tq:]
    y = d * lax.rsqrt(jnp.mean(d * d, axis=-1, keepdims=True) + EPS)
    o_ref[...] = ((y * g_ref[...]) * (1.0 - lam_init)).astype(o_ref.dtype)


def diff_attention_call(proj, tok0, b, t, lam_p, subln_g, lam_init, ctx=None, rope=None, tq=128):
    tq = min(tq, t)
    nq = t // tq
    q0 = tok0 // tq
    k0 = tok0 // t
    has_ctx = ctx is not None
    in_specs = [pl.BlockSpec((tq, LANES), lambda bi, h, qi: (q0 + bi * nq + qi, h)),
                pl.BlockSpec((t, LANES), lambda bi, h, qi: (k0 + bi, A_HEADS + h)),
                pl.BlockSpec((t, LANES), lambda bi, h, qi: (k0 + bi, 2 * A_HEADS + h))]
    args = [proj, proj, proj]
    if has_ctx:
        ck, cv = ctx
        lc = ck.shape[1]
        cos, sin = rope
        in_specs += [pl.BlockSpec((None, lc, LANES), lambda bi, h, qi: (bi, 0, h)),
                     pl.BlockSpec((None, lc, LANES), lambda bi, h, qi: (bi, 0, h)),
                     pl.BlockSpec((t, LANES), lambda bi, h, qi: (0, 0)),
                     pl.BlockSpec((t, LANES), lambda bi, h, qi: (0, 0))]
        args += [ck, cv, cos, sin]
    in_specs += [pl.BlockSpec((4, A_QK_DIM), lambda bi, h, qi: (0, 0)),
                 pl.BlockSpec((1, A_V_DIM), lambda bi, h, qi: (0, 0))]
    args += [lam_p, subln_g.reshape(1, A_V_DIM)]
    return pl.pallas_call(
        functools.partial(_diff_attn_kernel, has_ctx=has_ctx, lam_init=lam_init, tq=tq),
        grid=(b, A_HEADS, nq),
        in_specs=in_specs,
        out_specs=pl.BlockSpec((tq, LANES), lambda bi, h, qi: (bi * nq + qi, h)),
        out_shape=jax.ShapeDtypeStruct((b * t, A_WIDTH), BF16),
        scratch_shapes=[pltpu.VMEM((t, LANES), BF16), pltpu.VMEM((t, LANES), BF16)],
        compiler_params=_cparams(("arbitrary", "arbitrary", "arbitrary")),
        name="diff_attention",
    )(*args)


def rope_tables(t):
    tok = jnp.arange(t)
    row = (tok // GRID_W).astype(F32)
    col = (tok % GRID_W).astype(F32)
    n_freq = A_QK_DIM // 4
    inv = ROPE_BASE ** (-jnp.arange(n_freq, dtype=F32) / n_freq)
    ang_r = row[:, None] * inv
    ang_c = col[:, None] * inv
    cos64 = jnp.concatenate([jnp.cos(ang_r), jnp.cos(ang_r), jnp.cos(ang_c), jnp.cos(ang_c)], axis=1)
    sin64 = jnp.concatenate([-jnp.sin(ang_r), jnp.sin(ang_r), -jnp.sin(ang_c), jnp.sin(ang_c)], axis=1)
    return jnp.tile(cos64, (1, 2)), jnp.tile(sin64, (1, 2))


def _attn_kernel(q_ref, k_ref, v_ref, o_ref):
    q = q_ref[...].astype(BF16)
    k = k_ref[...].astype(BF16)
    s = lax.dot_general(q, k, NT_DIMS, preferred_element_type=F32) * (C_HEAD_DIM ** -0.5)
    mx = jnp.max(s, axis=-1, keepdims=True)
    e = jnp.exp(s - mx)
    l = jnp.sum(e, axis=-1, keepdims=True)
    o = jnp.dot(e.astype(BF16), v_ref[...].astype(BF16), preferred_element_type=F32)
    o_ref[...] = (o / l).astype(o_ref.dtype)


def attention_call(proj, b, t):
    return pl.pallas_call(
        _attn_kernel,
        grid=(b, C_HEADS),
        in_specs=[pl.BlockSpec((t, LANES), lambda bi, h: (bi, h)),
                  pl.BlockSpec((t, LANES), lambda bi, h: (bi, C_HEADS + h)),
                  pl.BlockSpec((t, LANES), lambda bi, h: (bi, 2 * C_HEADS + h))],
        out_specs=pl.BlockSpec((t, LANES), lambda bi, h: (bi, h)),
        out_shape=jax.ShapeDtypeStruct((b * t, C_WIDTH), BF16),
        compiler_params=_cparams(("arbitrary", "arbitrary")),
        name="ctx_attention",
    )(proj, proj, proj)


NA_QROWS = 8
NA_WROWS = 16
NA_CHUNK_ROWS = 4
NA_NCHUNK = NA_WROWS // NA_CHUNK_ROWS


def _na_window_start(rb, rows):
    return np.clip(rb * NA_QROWS - NA_KH // 2, 0, rows - NA_WROWS)


def na_bias_tables(rpb, rows):
    n_rb = rows // NA_QROWS
    geos = []
    for rb in (0, 1, n_rb - 1):
        r0 = rb * NA_QROWS
        ws = _na_window_start(rb, rows)
        qr = r0 + np.arange(NA_QROWS)[:, None, None, None]
        qc = np.arange(GRID_W)[None, :, None, None]
        kr = ws + np.arange(NA_WROWS)[None, None, :, None]
        kc = np.arange(GRID_W)[None, None, None, :]
        rs = np.clip(qr - NA_KH // 2, 0, rows - NA_KH)
        cs = np.clip(qc - NA_KW // 2, 0, GRID_W - NA_KW)
        ok = (kr >= rs) & (kr < rs + NA_KH) & (kc >= cs) & (kc < cs + NA_KW)
        ri = np.clip(kr - qr + NA_KH - 1, 0, 2 * NA_KH - 2)
        ci = np.clip(kc - qc + NA_KW - 1, 0, 2 * NA_KW - 2)
        shape = (NA_QROWS, GRID_W, NA_WROWS, GRID_W)
        ok = np.broadcast_to(ok, shape).reshape(NA_QROWS * GRID_W, NA_WROWS * GRID_W)
        ri = np.broadcast_to(ri, shape).reshape(ok.shape)
        ci = np.broadcast_to(ci, shape).reshape(ok.shape)
        geos.append(jnp.where(ok[None], rpb[:, ri, ci], -jnp.inf))
    return jnp.stack(geos, axis=1).astype(F32)


def _na_kernel(*refs):
    q_ref = refs[0]
    k_refs = refs[1:1 + NA_NCHUNK]
    v_refs = refs[1 + NA_NCHUNK:1 + 2 * NA_NCHUNK]
    ck_ref, cv_ref, bias_ref, o_ref = refs[1 + 2 * NA_NCHUNK:]
    scale = C_HEAD_DIM ** -0.5
    q = q_ref[...].astype(BF16)
    k = jnp.concatenate([r[...].astype(BF16) for r in k_refs], axis=0)
    v = jnp.concatenate([r[...].astype(BF16) for r in v_refs], axis=0)
    s = lax.dot_general(q, k, NT_DIMS, preferred_element_type=F32) * scale + bias_ref[...]
    sc = lax.dot_general(q, ck_ref[...].astype(BF16), NT_DIMS, preferred_element_type=F32) * scale
    mx = jnp.maximum(jnp.max(s, axis=-1, keepdims=True), jnp.max(sc, axis=-1, keepdims=True))
    e = jnp.exp(s - mx)
    ec = jnp.exp(sc - mx)
    l = jnp.sum(e, axis=-1, keepdims=True) + jnp.sum(ec, axis=-1, keepdims=True)
    o = (jnp.dot(e.astype(BF16), v, preferred_element_type=F32)
         + jnp.dot(ec.astype(BF16), cv_ref[...].astype(BF16), preferred_element_type=F32))
    o_ref[...] = (o / l).astype(o_ref.dtype)


def na_call(proj, tok0, b, t, ck, cv, bias):
    rows = t // GRID_W
    n_rb = rows // NA_QROWS
    qtok = NA_QROWS * GRID_W
    ctok = NA_CHUNK_ROWS * GRID_W
    lc = ck.shape[1]
    q0 = tok0 // qtok
    c0 = tok0 // ctok
    cps = t // ctok

    def wchunk(rb):
        return jnp.clip(rb * (NA_QROWS // NA_CHUNK_ROWS) - 1, 0, (rows - NA_WROWS) // NA_CHUNK_ROWS)

    def geo(rb):
        return jnp.where(rb == 0, 0, jnp.where(rb == n_rb - 1, 2, 1))

    def kspec(c, colblk):
        return pl.BlockSpec((ctok, LANES),
                            lambda h, rb, bi: (c0 + bi * cps + wchunk(rb) + c, colblk + h))

    in_specs = ([pl.BlockSpec((qtok, LANES), lambda h, rb, bi: (q0 + bi * n_rb + rb, h))]
                + [kspec(c, C_HEADS) for c in range(NA_NCHUNK)]
                + [kspec(c, 2 * C_HEADS) for c in range(NA_NCHUNK)]
                + [pl.BlockSpec((None, lc, LANES), lambda h, rb, bi: (bi, 0, h)),
                   pl.BlockSpec((None, lc, LANES), lambda h, rb, bi: (bi, 0, h)),
                   pl.BlockSpec((None, None, qtok, NA_WROWS * GRID_W),
                                lambda h, rb, bi: (h, geo(rb), 0, 0))])
    return pl.pallas_call(
        _na_kernel,
        grid=(C_HEADS, n_rb, b),
        in_specs=in_specs,
        out_specs=pl.BlockSpec((qtok, LANES), lambda h, rb, bi: (bi * n_rb + rb, h)),
        out_shape=jax.ShapeDtypeStruct((b * t, C_WIDTH), BF16),
        compiler_params=_cparams(("arbitrary", "arbitrary", "arbitrary")),
        name="neighbourhood_attention",
    )(*([proj] * (1 + 2 * NA_NCHUNK)), ck, cv, bias)


LRU_HALO = SUBLANES


def _expm1(x):
    u = jnp.exp(x)
    return jnp.where(u == 1.0, x, (u - 1.0) * x / jnp.log(u))


def _lru_gates_kernel(x_ref, prev_ref, next_ref, cw_ref, cb_ref, wa_ref, ba_ref, wi_ref, bi_ref,
                      lam_ref, af_ref, bf_ref, ab_ref, bb_ref, *, tt, n_prompt, t_p, t_s):
    i = pl.program_id(0)
    s = i * tt
    in_prompt = s < n_prompt
    pos = jnp.where(in_prompt, s % t_p, (s - n_prompt) % t_s)
    seq_len = jnp.where(in_prompt, t_p, t_s)
    is_start = pos == 0
    is_end = pos + tt == seq_len
    x = x_ref[...]
    prev = jnp.where(is_start, 0.0, prev_ref[LRU_HALO - 2:LRU_HALO, :])
    nxt = jnp.where(is_end, 0.0, next_ref[0:1, :])
    xp = jnp.concatenate([prev, x, nxt], axis=0)
    cw = cw_ref[...]
    xc = cb_ref[...] + xp[0:tt] * cw[0:1]
    for j in range(1, CONV_W):
        xc = xc + xp[j:j + tt] * cw[j:j + 1]
    outs = ((af_ref, bf_ref), (ab_ref, bb_ref))
    for n in range(LRU_BLOCKS):
        sl = slice(n * LRU_BLOCK, (n + 1) * LRU_BLOCK)
        xb = xc[:, sl]
        xb16 = xb.astype(BF16)
        for dr in range(2):
            r = _sigmoid(jnp.dot(xb16, wa_ref[dr, n], preferred_element_type=F32) + ba_ref[dr:dr + 1, sl])
            g = _sigmoid(jnp.dot(xb16, wi_ref[dr, n], preferred_element_type=F32) + bi_ref[dr:dr + 1, sl])
            z = -lam_ref[dr:dr + 1, sl]
            softplus = jnp.maximum(z, 0.0) + jnp.log(1.0 + jnp.exp(-jnp.abs(z)))
            log_a = -LRU_C * r * softplus
            a_out, b_out = outs[dr]
            a_out[:, sl] = jnp.exp(log_a)
            b_out[:, sl] = jnp.sqrt(-_expm1(2.0 * log_a)) * g * xb


def lru_gates(proj, conv_w, conv_b, w_a, b_a, w_i, b_i, lam, *, n_prompt, t_p, t_s, tt=256):
    n = proj.shape[0]
    xcol = (3 * A_WIDTH) // LRU_WIDTH
    hb = tt // LRU_HALO
    nh = n // LRU_HALO
    full = lambda shape: pl.BlockSpec(shape, lambda i: (0,) * len(shape))
    out = jax.ShapeDtypeStruct((n, LRU_WIDTH), F32)
    ospec = pl.BlockSpec((tt, LRU_WIDTH), lambda i: (i, 0))
    return pl.pallas_call(
        functools.partial(_lru_gates_kernel, tt=tt, n_prompt=n_prompt, t_p=t_p, t_s=t_s),
        grid=(n // tt,),
        in_specs=[pl.BlockSpec((tt, LRU_WIDTH), lambda i: (i, xcol)),
                  pl.BlockSpec((LRU_HALO, LRU_WIDTH), lambda i: (jnp.maximum(i * hb - 1, 0), xcol)),
                  pl.BlockSpec((LRU_HALO, LRU_WIDTH), lambda i: (jnp.minimum((i + 1) * hb, nh - 1), xcol)),
                  full((CONV_W, LRU_WIDTH)), full((1, LRU_WIDTH)),
                  full((2, LRU_BLOCKS, LRU_BLOCK, LRU_BLOCK)), full((2, LRU_WIDTH)),
                  full((2, LRU_BLOCKS, LRU_BLOCK, LRU_BLOCK)), full((2, LRU_WIDTH)),
                  full((2, LRU_WIDTH))],
        out_specs=[ospec] * 4,
        out_shape=[out] * 4,
        compiler_params=_cparams(("arbitrary",)),
        name="lru_gates",
    )(proj, proj, proj, conv_w, conv_b.reshape(1, LRU_WIDTH), w_a.astype(BF16), b_a,
      w_i.astype(BF16), b_i, lam)


def _lru_scan_kernel(*refs, tt, reverse, combine):
    if combine:
        a_ref, b_ref, h0_ref, hf_ref, g_ref, o_ref, fin_ref, h_scr = refs
    else:
        a_ref, b_ref, h0_ref, o_ref, fin_ref, h_scr = refs
    ti = pl.program_id(1)

    @pl.when(ti == 0)
    def _():
        h_scr[...] = h0_ref[...]

    def step(j, h):
        t = tt - 1 - j if reverse else j
        h = a_ref[t] * h + b_ref[t]
        if combine:
            o_ref[t] = ((hf_ref[t] + h) * _gelu(g_ref[t])).astype(o_ref.dtype)
        else:
            o_ref[t] = h
        return h

    h = lax.fori_loop(0, tt, step, h_scr[...], unroll=8)
    h_scr[...] = h
    fin_ref[...] = h


def lru_scan(a, b, h0, tok0, nb, t, *, reverse, hf=None, proj=None, tt=256):
    tt = min(tt, t)
    nt = t // tt
    r0 = tok0 // tt
    combine = hf is not None
    vreg = (SUBLANES, LANES)
    view = lambda z: z.reshape(z.shape[0], LRU_WIDTH // LANES, LANES)

    def tmap(bi, ti):
        return nt - 1 - ti if reverse else ti

    in_specs = [pl.BlockSpec((tt,) + vreg, lambda bi, ti: (r0 + bi * nt + tmap(bi, ti), 0, 0)),
                pl.BlockSpec((tt,) + vreg, lambda bi, ti: (r0 + bi * nt + tmap(bi, ti), 0, 0)),
                pl.BlockSpec((None,) + vreg, lambda bi, ti: (bi, 0, 0))]
    args = [view(a), view(b), view(h0)]
    if combine:
        gcol = (3 * A_WIDTH + LRU_WIDTH) // LRU_WIDTH
        in_specs += [pl.BlockSpec((tt,) + vreg, lambda bi, ti: (bi * nt + tmap(bi, ti), 0, 0)),
                     pl.BlockSpec((tt,) + vreg, lambda bi, ti: (r0 + bi * nt + tmap(bi, ti), gcol, 0))]
        args += [view(hf), proj.reshape(proj.shape[0], EVEN_IN // LANES, LANES)]
    out, fin = pl.pallas_call(
        functools.partial(_lru_scan_kernel, tt=tt, reverse=reverse, combine=combine),
        grid=(nb, nt),
        in_specs=in_specs,
        out_specs=[pl.BlockSpec((tt,) + vreg, lambda bi, ti: (bi * nt + tmap(bi, ti), 0, 0)),
                   pl.BlockSpec((None,) + vreg, lambda bi, ti: (bi, 0, 0))],
        out_shape=[jax.ShapeDtypeStruct((nb * t,) + vreg, F32),
                   jax.ShapeDtypeStruct((nb,) + vreg, F32)],
        scratch_shapes=[pltpu.VMEM(vreg, F32)],
        compiler_params=_cparams(("arbitrary", "arbitrary")),
        name="lru_scan_bwd" if reverse else "lru_scan_fwd",
    )(*args)
    return out.reshape(nb * t, LRU_WIDTH), fin.reshape(nb, LRU_WIDTH)


def _topk_rows(s, k, payload=None):
    n = s.shape[0]
    iota = lax.broadcasted_iota(jnp.int32, s.shape, 0).astype(F32)
    vals, idxs = [], []
    for _ in range(k):
        m = jnp.max(s, axis=0, keepdims=True)
        am = jnp.min(jnp.where(s == m, iota, float(n)), axis=0, keepdims=True)
        sel = iota == am
        vals.append(m)
        if payload is None:
            idxs.append(am)
        else:
            idxs.append(jnp.sum(jnp.where(sel, payload, 0.0), axis=0, keepdims=True))
        s = jnp.where(sel, -jnp.inf, s)
    return jnp.concatenate(vals, axis=0), jnp.concatenate(idxs, axis=0)


def _peer_route_kernel(q_ref, sk_ref, idx_ref, gate_ref, *, tm):
    half = PEER_QDIM // 2
    for c in range(tm // LANES):
        q = q_ref[c * LANES:(c + 1) * LANES, :].astype(BF16)
        tops = []
        for p in range(2):
            s = lax.dot_general(sk_ref[p], q[:, p * half:(p + 1) * half], NT_DIMS,
                                preferred_element_type=F32)
            tops.append(_topk_rows(s, PEER_TOPK))
        (s1, i1), (s2, i2) = tops
        cand = jnp.concatenate([s1[a:a + 1] + s2 for a in range(PEER_TOPK)], axis=0)
        cidx = jnp.concatenate([i1[a:a + 1] * float(PEER_NKEYS) + i2 for a in range(PEER_TOPK)], axis=0)
        bs, bidx = _topk_rows(cand, PEER_TOPK, payload=cidx)
        e = jnp.exp(bs - bs[0:1])
        gates = e / jnp.sum(e, axis=0, keepdims=True)
        idx_ref[:, c * LANES:(c + 1) * LANES] = bidx.astype(jnp.int32)
        gate_ref[:, c * LANES:(c + 1) * LANES] = gates


def peer_route(q, sub_keys, tm=256):
    n = q.shape[0]
    half = PEER_QDIM // 2
    return pl.pallas_call(
        functools.partial(_peer_route_kernel, tm=tm),
        grid=(n // tm, PEER_HEADS),
        in_specs=[pl.BlockSpec((tm, PEER_QDIM), lambda i, h: (i, h)),
                  pl.BlockSpec((None, 2, PEER_NKEYS, half), lambda i, h: (h, 0, 0, 0))],
        out_specs=[pl.BlockSpec((None, PEER_TOPK, tm), lambda i, h: (h, 0, i))] * 2,
        out_shape=[jax.ShapeDtypeStruct((PEER_HEADS, PEER_TOPK, n), jnp.int32),
                   jax.ShapeDtypeStruct((PEER_HEADS, PEER_TOPK, n), F32)],
        compiler_params=_cparams(("arbitrary", "arbitrary")),
        name="peer_route",
    )(q, sub_keys)


def _peer_gates_kernel(idx_ref, gate_ref, g_ref):
    idx = idx_ref[...]
    gate = gate_ref[...]
    tg, _, hk = idx.shape
    i1 = idx // PEER_NKEYS
    i2 = idx % PEER_NKEYS
    iota = lax.broadcasted_iota(jnp.int32, (tg, PEER_NKEYS, hk), 1)
    a = jnp.where(i1 == iota, gate, 0.0)
    a_hi = a.astype(BF16)
    a_lo = (a - a_hi.astype(F32)).astype(BF16)
    bsel = jnp.where(i2 == iota, 1.0, 0.0).astype(BF16)
    dims = (((2,), (2,)), ((0,), (0,)))
    g_ref[...] = (lax.dot_general(a_hi, bsel, dims, preferred_element_type=F32)
                  + lax.dot_general(a_lo, bsel, dims, preferred_element_type=F32))


def peer_gate_rows(idx, gates, tg=64):
    n, hk = idx.shape
    g = pl.pallas_call(
        _peer_gates_kernel,
        grid=(n // tg,),
        in_specs=[pl.BlockSpec((tg, 1, hk), lambda i: (i, 0, 0))] * 2,
        out_specs=pl.BlockSpec((tg, PEER_NKEYS, PEER_NKEYS), lambda i: (i, 0, 0)),
        out_shape=jax.ShapeDtypeStruct((n, PEER_NKEYS, PEER_NKEYS), F32),
        compiler_params=_cparams(("arbitrary",)),
        name="peer_gate_rows",
    )(idx.reshape(n, 1, hk), gates.reshape(n, 1, hk))
    return g.reshape(n, PEER_N)


def _peer_dense_kernel(h_ref, u_ref, v_ref, gr_ref, x_ref, mg_ref, fg_ref, o_ref, acc_ref, *,
                       final_norm):
    e = pl.program_id(1)

    @pl.when(e == 0)
    def _():
        acc_ref[...] = jnp.zeros_like(acc_ref)

    s = lax.dot_general(h_ref[...], u_ref[...], NT_DIMS, preferred_element_type=F32)
    act = (_gelu(s) * gr_ref[...]).astype(BF16)
    acc_ref[...] += jnp.dot(act, v_ref[...], preferred_element_type=F32)

    @pl.when(e == pl.num_programs(1) - 1)
    def _():
        y = x_ref[...] + mg_ref[...] * acc_ref[...]
        if final_norm:
            y = y * lax.rsqrt(jnp.mean(y * y, axis=-1, keepdims=True) + EPS) * fg_ref[...]
        o_ref[...] = y


def peer_dense(h, u16, v16, grows, x, mod, gate_idx, final_g, *, n_prompt, t_s, final_norm,
               tm=512, te=512):
    n, d = x.shape
    ne = u16.shape[0]
    row = functools.partial(_mod_row, tm=tm, n_prompt=n_prompt, t_s=t_s)
    return pl.pallas_call(
        functools.partial(_peer_dense_kernel, final_norm=final_norm),
        grid=(n // tm, ne // te),
        in_specs=[pl.BlockSpec((tm, d), lambda i, e: (i, 0)),
                  pl.BlockSpec((te, d), lambda i, e: (e, 0)),
                  pl.BlockSpec((te, d), lambda i, e: (e, 0)),
                  pl.BlockSpec((tm, te), lambda i, e: (i, e)),
                  pl.BlockSpec((tm, d), lambda i, e: (i, 0)),
                  pl.BlockSpec((None, None, 1, d), lambda i, e: (row(i), gate_idx, 0, 0)),
                  pl.BlockSpec((1, d), lambda i, e: (0, 0))],
        out_specs=pl.BlockSpec((tm, d), lambda i, e: (i, 0)),
        out_shape=jax.ShapeDtypeStruct((n, d), F32),
        scratch_shapes=[pltpu.VMEM((tm, d), F32)],
        compiler_params=_cparams(("arbitrary", "arbitrary")),
        name="peer_dense",
    )(h, u16, v16, grows, x, mod, final_g.reshape(1, d))


def peer_block(x, norm_g, mod, w_q16, sk16, u16, v16, final_g, *, n_prompt, t_s, final_norm):
    n = x.shape[0]
    q, h = norm_matmul(x, norm_g, mod, 3, 4, w_q16, n_prompt=n_prompt, t_s=t_s, emit_h=True)
    idx, gates = peer_route(q, sk16)
    hk = PEER_HEADS * PEER_TOPK
    idx = idx.reshape(hk, n).T
    gates = gates.reshape(hk, n).T
    grows = peer_gate_rows(idx, gates)
    return peer_dense(h, u16, v16, grows, x, mod, 5, final_g, n_prompt=n_prompt, t_s=t_s,
                      final_norm=final_norm)


def kernel(x_prompt, x_sample, cache_a_k, cache_a_v, state_lru, cache_c_k, cache_c_v, c, c_ctx, w_mod, b_mod, norm1_g, norm2_g, final_norm_g, even_w_in, even_w_out, a_lambda, a_subln_g, lru_conv_w, lru_conv_b, lru_w_a, lru_b_a, lru_w_i, lru_b_i, lru_lambda, odd_w_in, odd_w_out, na_rpb, peer_w_q, peer_sub_keys, peer_u, peer_v):
    b_p, t_p, d = x_prompt.shape
    b_s, t_s, _ = x_sample.shape
    depth = w_mod.shape[0]
    n_p = b_p * t_p
    n_s = b_s * t_s
    lc = cache_a_k.shape[2]
    tok = dict(n_prompt=n_p, t_s=t_s)

    x = jnp.concatenate([x_prompt.reshape(n_p, d), x_sample.reshape(n_s, d)], axis=0)
    n_rows = -(-(1 + b_s) // SUBLANES) * SUBLANES
    cond = jnp.concatenate([c_ctx[None], c, jnp.zeros((n_rows - 1 - b_s, d), F32)], axis=0)
    mod_all = modulation_all(cond, w_mod, b_mod).reshape(depth, n_rows, 6, 1, d)
    rope = rope_tables(t_s)

    new_ak, new_av, new_lru, new_ck, new_cv = [], [], [], [], []
    for l in range(depth):
        mod = mod_all[l]
        if l % 2 == 0:
            e = l // 2
            lam_init = 0.8 - 0.6 * math.exp(-0.3 * l)
            proj = norm_matmul(x, norm1_g[l], mod, 0, 1, even_w_in[e].astype(BF16), **tok)
            ctx = (cache_a_k[:, e].reshape(b_s, lc, A_WIDTH), cache_a_v[:, e].reshape(b_s, lc, A_WIDTH))
            oa_p = diff_attention_call(proj, 0, b_p, t_p, a_lambda[e], a_subln_g[e], lam_init)
            oa_s = diff_attention_call(proj, n_p, b_s, t_s, a_lambda[e], a_subln_g[e], lam_init,
                                       ctx=ctx, rope=rope)
            a_f, b_f, a_b, b_b = lru_gates(proj, lru_conv_w[e], lru_conv_b[e], lru_w_a[e], lru_b_a[e],
                                           lru_w_i[e], lru_b_i[e], lru_lambda[e],
                                           n_prompt=n_p, t_p=t_p, t_s=t_s)
            zeros_p = jnp.zeros((b_p, LRU_WIDTH), F32)
            hf_p, fin_f = lru_scan(a_f, b_f, zeros_p, 0, b_p, t_p, reverse=False)
            ob_p, fin_b = lru_scan(a_b, b_b, zeros_p, 0, b_p, t_p, reverse=True, hf=hf_p, proj=proj)
            hf_s, _ = lru_scan(a_f, b_f, state_lru[:, e, 0], n_p, b_s, t_s, reverse=False)
            ob_s, _ = lru_scan(a_b, b_b, state_lru[:, e, 1], n_p, b_s, t_s, reverse=True,
                               hf=hf_s, proj=proj)
            o_a = jnp.concatenate([oa_p, oa_s], axis=0)
            o_b = jnp.concatenate([ob_p, ob_s], axis=0)
            w_out = even_w_out[e].astype(BF16)
            x = matmul_res([o_a, o_b], [w_out[:A_WIDTH], w_out[A_WIDTH:]], x, mod, 2, **tok)
            new_ak.append(proj[:n_p, A_WIDTH:2 * A_WIDTH].reshape(b_p, t_p, A_HEADS, 2 * A_QK_DIM))
            new_av.append(proj[:n_p, 2 * A_WIDTH:3 * A_WIDTH].reshape(b_p, t_p, A_HEADS, A_V_DIM))
            new_lru.append(jnp.stack([fin_f, fin_b], axis=1))
        else:
            o = l // 2
            proj = norm_matmul(x, norm1_g[l], mod, 0, 1, odd_w_in[o].astype(BF16), **tok)
            oc_p = attention_call(proj, b_p, t_p)
            bias = na_bias_tables(na_rpb[o], t_s // GRID_W)
            oc_s = na_call(proj, n_p, b_s, t_s, cache_c_k[:, o].reshape(b_s, lc, C_WIDTH),
                           cache_c_v[:, o].reshape(b_s, lc, C_WIDTH), bias)
            o_c = jnp.concatenate([oc_p, oc_s], axis=0)
            x = matmul_res([o_c], [odd_w_out[o].astype(BF16)], x, mod, 2, **tok)
            new_ck.append(proj[:n_p, C_WIDTH:2 * C_WIDTH].reshape(b_p, t_p, C_HEADS, C_HEAD_DIM))
            new_cv.append(proj[:n_p, 2 * C_WIDTH:3 * C_WIDTH].reshape(b_p, t_p, C_HEADS, C_HEAD_DIM))
        x = peer_block(x, norm2_g[l], mod, peer_w_q[l].astype(BF16), peer_sub_keys[l].astype(BF16),
                       peer_u[l].astype(BF16), peer_v[l].astype(BF16), final_norm_g,
                       final_norm=(l == depth - 1), **tok)
    y_prompt = x[:n_p].reshape(b_p, t_p, d)
    y_sample = x[n_p:].reshape(b_s, t_s, d)
    return (y_prompt, y_sample, jnp.stack(new_ak, axis=1), jnp.stack(new_av, axis=1),
            jnp.stack(new_lru, axis=1), jnp.stack(new_ck, axis=1), jnp.stack(new_cv, axis=1))
```

```python
import functools
import math

import numpy as np
import jax
import jax.numpy as jnp
from jax import lax
from jax.experimental import pallas as pl
from jax.experimental.pallas import tpu as pltpu

F32 = jnp.float32
BF16 = jnp.bfloat16

D_MODEL = 2048
GRID_W = 64
EPS = 1e-6
ROPE_BASE = 10000.0
A_HEADS = 8
A_QK_DIM = 64
A_V_DIM = 128
A_WIDTH = A_HEADS * A_V_DIM
LRU_WIDTH = 1024
LRU_BLOCKS = 8
LRU_BLOCK = LRU_WIDTH // LRU_BLOCKS
CONV_W = 4
LRU_C = 8.0
EVEN_IN = 3 * A_WIDTH + 2 * LRU_WIDTH
C_HEADS = 16
C_HEAD_DIM = 128
C_WIDTH = C_HEADS * C_HEAD_DIM
NA_KH = 8
NA_KW = 16
PEER_HEADS = 8
PEER_NKEYS = 128
PEER_N = PEER_NKEYS * PEER_NKEYS
PEER_QDIM = 256
PEER_TOPK = 16

LANES = 128
SUBLANES = 8
VMEM_LIMIT = 56 * 1024 * 1024

NT_DIMS = (((1,), (1,)), ((), ()))


def _cparams(sem):
    return pltpu.CompilerParams(dimension_semantics=sem, vmem_limit_bytes=VMEM_LIMIT)


def _gelu(x):
    c = math.sqrt(2.0 / math.pi)
    return 0.5 * x * (1.0 + jnp.tanh(c * (x + 0.044715 * (x * x * x))))


def _sigmoid(x):
    return 1.0 / (1.0 + jnp.exp(-x))


def _mod_kernel(c_ref, w_ref, b_ref, o_ref):
    c = c_ref[...]
    s = c * _sigmoid(c)
    o_ref[0] = jnp.dot(s.astype(BF16), w_ref[0].astype(BF16),
                       preferred_element_type=F32) + b_ref[0]


def modulation_all(cond, w_mod, b_mod):
    depth, d, n6 = w_mod.shape
    r = cond.shape[0]
    tn = 768
    return pl.pallas_call(
        _mod_kernel,
        grid=(depth, n6 // tn),
        in_specs=[pl.BlockSpec((r, d), lambda l, j: (0, 0)),
                  pl.BlockSpec((1, d, tn), lambda l, j: (l, 0, j)),
                  pl.BlockSpec((1, 1, tn), lambda l, j: (l, 0, j))],
        out_specs=pl.BlockSpec((1, r, tn), lambda l, j: (l, 0, j)),
        out_shape=jax.ShapeDtypeStruct((depth, r, n6), F32),
        compiler_params=_cparams(("arbitrary", "arbitrary")),
        name="modulation",
    )(cond, w_mod, b_mod.reshape(depth, 1, n6))


def _mod_row(i, tm, n_prompt, t_s):
    s = i * tm
    return jnp.where(s < n_prompt, 0, 1 + (s - n_prompt) // t_s)


def _norm_matmul_kernel(x_ref, g_ref, sh_ref, sc_ref, w_ref, o_ref, *rest, emit_h):
    if emit_h:
        h_out_ref, h_scr = rest
    else:
        (h_scr,) = rest

    @pl.when(pl.program_id(1) == 0)
    def _():
        x = x_ref[...]
        y = x * lax.rsqrt(jnp.mean(x * x, axis=-1, keepdims=True) + EPS)
        h = (y * g_ref[...]) * (1.0 + sc_ref[...]) + sh_ref[...]
        h_scr[...] = h.astype(BF16)
        if emit_h:
            h_out_ref[...] = h.astype(BF16)

    o_ref[...] = jnp.dot(h_scr[...], w_ref[...], preferred_element_type=F32).astype(o_ref.dtype)


def norm_matmul(x, g, mod, shift_idx, scale_idx, w, *, n_prompt, t_s, emit_h=False,
                tm=1024, tn=512, out_dtype=F32):
    n, d = x.shape
    nout = w.shape[1]
    row = functools.partial(_mod_row, tm=tm, n_prompt=n_prompt, t_s=t_s)
    out_shape = [jax.ShapeDtypeStruct((n, nout), out_dtype)]
    out_specs = [pl.BlockSpec((tm, tn), lambda i, j: (i, j))]
    if emit_h:
        out_shape.append(jax.ShapeDtypeStruct((n, d), BF16))
        out_specs.append(pl.BlockSpec((tm, d), lambda i, j: (i, 0)))
    res = pl.pallas_call(
        functools.partial(_norm_matmul_kernel, emit_h=emit_h),
        grid=(n // tm, nout // tn),
        in_specs=[pl.BlockSpec((tm, d), lambda i, j: (i, 0)),
                  pl.BlockSpec((1, d), lambda i, j: (0, 0)),
                  pl.BlockSpec((None, None, 1, d), lambda i, j: (row(i), shift_idx, 0, 0)),
                  pl.BlockSpec((None, None, 1, d), lambda i, j: (row(i), scale_idx, 0, 0)),
                  pl.BlockSpec((d, tn), lambda i, j: (0, j))],
        out_specs=out_specs,
        out_shape=out_shape,
        scratch_shapes=[pltpu.VMEM((tm, d), BF16)],
        compiler_params=_cparams(("arbitrary", "arbitrary")),
        name="norm_matmul",
    )(x, g.reshape(1, d), mod, mod, w)
    return res if emit_h else res[0]


def _matmul_res_kernel(*refs, n_in):
    a_refs = refs[:n_in]
    w_refs = refs[n_in:2 * n_in]
    x_ref, gate_ref, o_ref = refs[2 * n_in:]
    acc = None
    for a_ref, w_ref in zip(a_refs, w_refs):
        p = jnp.dot(a_ref[...].astype(BF16), w_ref[...], preferred_element_type=F32)
        acc = p if acc is None else acc + p
    o_ref[...] = x_ref[...] + gate_ref[...] * acc


def matmul_res(a_list, w_list, x, mod, gate_idx, *, n_prompt, t_s, tm=512, tn=512):
    n, d = x.shape
    n_in = len(a_list)
    row = functools.partial(_mod_row, tm=tm, n_prompt=n_prompt, t_s=t_s)
    in_specs = ([pl.BlockSpec((tm, a.shape[1]), lambda i, j: (i, 0)) for a in a_list]
                + [pl.BlockSpec((w.shape[0], tn), lambda i, j: (0, j)) for w in w_list]
                + [pl.BlockSpec((tm, tn), lambda i, j: (i, j)),
                   pl.BlockSpec((None, None, 1, tn), lambda i, j: (row(i), gate_idx, 0, j))])
    return pl.pallas_call(
        functools.partial(_matmul_res_kernel, n_in=n_in),
        grid=(n // tm, d // tn),
        in_specs=in_specs,
        out_specs=pl.BlockSpec((tm, tn), lambda i, j: (i, j)),
        out_shape=jax.ShapeDtypeStruct((n, d), F32),
        compiler_params=_cparams(("arbitrary", "arbitrary")),
        name="matmul_res",
    )(*a_list, *w_list, x, mod)


def _rope(x, cos, sin_signed):
    lane = lax.broadcasted_iota(jnp.int32, x.shape, 1)
    first = (lane % 32) < 16
    partner = jnp.where(first, pltpu.roll(x, LANES - 16, axis=1), pltpu.roll(x, 16, axis=1))
    return x * cos + partner * sin_signed


def _diff_attn_kernel(*refs, has_ctx, lam_init, tq):
    if has_ctx:
        (q_ref, k_ref, v_ref, ck_ref, cv_ref, cos_ref, sin_ref, lam_ref, g_ref,
         o_ref, k_scr, v_scr) = refs
    else:
        q_ref, k_ref, v_ref, lam_ref, g_ref, o_ref, k_scr, v_scr = refs
    qi = pl.program_id(2)

    @pl.when(qi == 0)
    def _():
        k = k_ref[...]
        if has_ctx:
            k = _rope(k, cos_ref[...], sin_ref[...])
        k_scr[...] = k.astype(BF16)
        v_scr[...] = v_ref[...].astype(BF16)

    q = q_ref[...]
    if has_ctx:
        off = pl.multiple_of(qi * tq, tq)
        q = _rope(q, cos_ref[pl.ds(off, tq), :], sin_ref[pl.ds(off, tq), :])
    q = q * (A_QK_DIM ** -0.5)
    lane = lax.broadcasted_iota(jnp.int32, q.shape, 1)
    q2 = jnp.concatenate([jnp.where(lane < A_QK_DIM, q, 0.0),
                          jnp.where(lane >= A_QK_DIM, q, 0.0)], axis=0).astype(BF16)
    s = lax.dot_general(q2, k_scr[...], NT_DIMS, preferred_element_type=F32)
    mx = jnp.max(s, axis=-1, keepdims=True)
    if has_ctx:
        sc = lax.dot_general(q2, ck_ref[...].astype(BF16), NT_DIMS, preferred_element_type=F32)
        mx = jnp.maximum(mx, jnp.max(sc, axis=-1, keepdims=True))
    e = jnp.exp(s - mx)
    l = jnp.sum(e, axis=-1, keepdims=True)
    o = jnp.dot(e.astype(BF16), v_scr[...], preferred_element_type=F32)
    if has_ctx:
        ec = jnp.exp(sc - mx)
        l = l + jnp.sum(ec, axis=-1, keepdims=True)
        o = o + jnp.dot(ec.astype(BF16), cv_ref[...].astype(BF16), preferred_element_type=F32)
    o = o / l
    lp = lam_ref[...]
    lam = (jnp.exp(jnp.sum(lp[0:1] * lp[1:2], axis=-1, keepdims=True))
           - jnp.exp(jnp.sum(lp[2:3] * lp[3:4], axis=-1, keepdims=True)) + lam_init)
    d = o[:tq] - lam * o[tq:]
    y = d * lax.rsqrt(jnp.mean(d * d, axis=-1, keepdims=True) + EPS)
    o_ref[...] = ((y * g_ref[...]) * (1.0 - lam_init)).astype(o_ref.dtype)


def diff_attention_call(proj, tok0, b, t, lam_p, subln_g, lam_init, ctx=None, rope=None, tq=256):
    tq = min(tq, t)
    nq = t // tq
    q0 = tok0 // tq
    k0 = tok0 // t
    has_ctx = ctx is not None
    in_specs = [pl.BlockSpec((tq, LANES), lambda bi, h, qi: (q0 + bi * nq + qi, h)),
                pl.BlockSpec((t, LANES), lambda bi, h, qi: (k0 + bi, A_HEADS + h)),
                pl.BlockSpec((t, LANES), lambda bi, h, qi: (k0 + bi, 2 * A_HEADS + h))]
    args = [proj, proj, proj]
    if has_ctx:
        ck, cv = ctx
        lc = ck.shape[1]
        cos, sin = rope
        in_specs += [pl.BlockSpec((None, lc, LANES), lambda bi, h, qi: (bi, 0, h)),
                     pl.BlockSpec((None, lc, LANES), lambda bi, h, qi: (bi, 0, h)),
                     pl.BlockSpec((t, LANES), lambda bi, h, qi: (0, 0)),
                     pl.BlockSpec((t, LANES), lambda bi, h, qi: (0, 0))]
        args += [ck, cv, cos, sin]
    in_specs += [pl.BlockSpec((4, A_QK_DIM), lambda bi, h, qi: (0, 0)),
                 pl.BlockSpec((1, A_V_DIM), lambda bi, h, qi: (0, 0))]
    args += [lam_p, subln_g.reshape(1, A_V_DIM)]
    return pl.pallas_call(
        functools.partial(_diff_attn_kernel, has_ctx=has_ctx, lam_init=lam_init, tq=tq),
        grid=(b, A_HEADS, nq),
        in_specs=in_specs,
        out_specs=pl.BlockSpec((tq, LANES), lambda bi, h, qi: (bi * nq + qi, h)),
        out_shape=jax.ShapeDtypeStruct((b * t, A_WIDTH), BF16),
        scratch_shapes=[pltpu.VMEM((t, LANES), BF16), pltpu.VMEM((t, LANES), BF16)],
        compiler_params=_cparams(("arbitrary", "arbitrary", "arbitrary")),
        name="diff_attention",
    )(*args)


def rope_tables(t):
    tok = jnp.arange(t)
    row = (tok // GRID_W).astype(F32)
    col = (tok % GRID_W).astype(F32)
    n_freq = A_QK_DIM // 4
    inv = ROPE_BASE ** (-jnp.arange(n_freq, dtype=F32) / n_freq)
    ang_r = row[:, None] * inv
    ang_c = col[:, None] * inv
    cos64 = jnp.concatenate([jnp.cos(ang_r), jnp.cos(ang_r), jnp.cos(ang_c), jnp.cos(ang_c)], axis=1)
    sin64 = jnp.concatenate([-jnp.sin(ang_r), jnp.sin(ang_r), -jnp.sin(ang_c), jnp.sin(ang_c)], axis=1)
    return jnp.tile(cos64, (1, 2)), jnp.tile(sin64, (1, 2))


def _attn_kernel(q_ref, k_ref, v_ref, o_ref):
    q = q_ref[...].astype(BF16)
    k = k_ref[...].astype(BF16)
    s = lax.dot_general(q, k, NT_DIMS, preferred_element_type=F32) * (C_HEAD_DIM ** -0.5)
    mx = jnp.max(s, axis=-1, keepdims=True)
    e = jnp.exp(s - mx)
    l = jnp.sum(e, axis=-1, keepdims=True)
    o = jnp.dot(e.astype(BF16), v_ref[...].astype(BF16), preferred_element_type=F32)
    o_ref[...] = (o / l).astype(o_ref.dtype)


def attention_call(proj, b, t):
    return pl.pallas_call(
        _attn_kernel,
        grid=(b, C_HEADS),
        in_specs=[pl.BlockSpec((t, LANES), lambda bi, h: (bi, h)),
                  pl.BlockSpec((t, LANES), lambda bi, h: (bi, C_HEADS + h)),
                  pl.BlockSpec((t, LANES), lambda bi, h: (bi, 2 * C_HEADS + h))],
        out_specs=pl.BlockSpec((t, LANES), lambda bi, h: (bi, h)),
        out_shape=jax.ShapeDtypeStruct((b * t, C_WIDTH), BF16),
        compiler_params=_cparams(("arbitrary", "arbitrary")),
        name="ctx_attention",
    )(proj, proj, proj)


NA_QROWS = 8
NA_WROWS = 16
NA_CHUNK_ROWS = 4
NA_NCHUNK = NA_WROWS // NA_CHUNK_ROWS


def _na_window_start(rb, rows):
    return np.clip(rb * NA_QROWS - NA_KH // 2, 0, rows - NA_WROWS)


def na_bias_tables(rpb, rows):
    n_rb = rows // NA_QROWS
    geos = []
    for rb in (0, 1, n_rb - 1):
        r0 = rb * NA_QROWS
        ws = _na_window_start(rb, rows)
        qr = r0 + np.arange(NA_QROWS)[:, None, None, None]
        qc = np.arange(GRID_W)[None, :, None, None]
        kr = ws + np.arange(NA_WROWS)[None, None, :, None]
        kc = np.arange(GRID_W)[None, None, None, :]
        rs = np.clip(qr - NA_KH // 2, 0, rows - NA_KH)
        cs = np.clip(qc - NA_KW // 2, 0, GRID_W - NA_KW)
        ok = (kr >= rs) & (kr < rs + NA_KH) & (kc >= cs) & (kc < cs + NA_KW)
        ri = np.clip(kr - qr + NA_KH - 1, 0, 2 * NA_KH - 2)
        ci = np.clip(kc - qc + NA_KW - 1, 0, 2 * NA_KW - 2)
        shape = (NA_QROWS, GRID_W, NA_WROWS, GRID_W)
        ok = np.broadcast_to(ok, shape).reshape(NA_QROWS * GRID_W, NA_WROWS * GRID_W)
        by_col = rpb[:, :, ci[0, :, 0, :]]
        tab = by_col[:, ri[:, 0, :, 0]]
        tab = tab.transpose(0, 1, 3, 2, 4).reshape((rpb.shape[0],) + ok.shape)
        geos.append(jnp.where(ok[None], tab, -jnp.inf))
    return jnp.stack(geos, axis=1).astype(F32)


def _na_kernel(*refs):
    q_ref = refs[0]
    k_refs = refs[1:1 + NA_NCHUNK]
    v_refs = refs[1 + NA_NCHUNK:1 + 2 * NA_NCHUNK]
    ck_ref, cv_ref, bias_ref, o_ref = refs[1 + 2 * NA_NCHUNK:]
    scale = C_HEAD_DIM ** -0.5
    q = q_ref[...].astype(BF16)
    k = jnp.concatenate([r[...].astype(BF16) for r in k_refs], axis=0)
    v = jnp.concatenate([r[...].astype(BF16) for r in v_refs], axis=0)
    s = lax.dot_general(q, k, NT_DIMS, preferred_element_type=F32) * scale + bias_ref[...]
    sc = lax.dot_general(q, ck_ref[...].astype(BF16), NT_DIMS, preferred_element_type=F32) * scale
    mx = jnp.maximum(jnp.max(s, axis=-1, keepdims=True), jnp.max(sc, axis=-1, keepdims=True))
    e = jnp.exp(s - mx)
    ec = jnp.exp(sc - mx)
    l = jnp.sum(e, axis=-1, keepdims=True) + jnp.sum(ec, axis=-1, keepdims=True)
    o = (jnp.dot(e.astype(BF16), v, preferred_element_type=F32)
         + jnp.dot(ec.astype(BF16), cv_ref[...].astype(BF16), preferred_element_type=F32))
    o_ref[...] = (o / l).astype(o_ref.dtype)


def na_call(proj, tok0, b, t, ck, cv, bias):
    rows = t // GRID_W
    n_rb = rows // NA_QROWS
    qtok = NA_QROWS * GRID_W
    ctok = NA_CHUNK_ROWS * GRID_W
    lc = ck.shape[1]
    q0 = tok0 // qtok
    c0 = tok0 // ctok
    cps = t // ctok

    def wchunk(rb):
        return jnp.clip(rb * (NA_QROWS // NA_CHUNK_ROWS) - 1, 0, (rows - NA_WROWS) // NA_CHUNK_ROWS)

    def geo(rb):
        return jnp.where(rb == 0, 0, jnp.where(rb == n_rb - 1, 2, 1))

    def kspec(c, colblk):
        return pl.BlockSpec((ctok, LANES),
                            lambda h, rb, bi: (c0 + bi * cps + wchunk(rb) + c, colblk + h))

    in_specs = ([pl.BlockSpec((qtok, LANES), lambda h, rb, bi: (q0 + bi * n_rb + rb, h))]
                + [kspec(c, C_HEADS) for c in range(NA_NCHUNK)]
                + [kspec(c, 2 * C_HEADS) for c in range(NA_NCHUNK)]
                + [pl.BlockSpec((None, lc, LANES), lambda h, rb, bi: (bi, 0, h)),
                   pl.BlockSpec((None, lc, LANES), lambda h, rb, bi: (bi, 0, h)),
                   pl.BlockSpec((None, None, qtok, NA_WROWS * GRID_W),
                                lambda h, rb, bi: (h, geo(rb), 0, 0))])
    return pl.pallas_call(
        _na_kernel,
        grid=(C_HEADS, n_rb, b),
        in_specs=in_specs,
        out_specs=pl.BlockSpec((qtok, LANES), lambda h, rb, bi: (bi * n_rb + rb, h)),
        out_shape=jax.ShapeDtypeStruct((b * t, C_WIDTH), BF16),
        compiler_params=_cparams(("arbitrary", "arbitrary", "arbitrary")),
        name="neighbourhood_attention",
    )(*([proj] * (1 + 2 * NA_NCHUNK)), ck, cv, bias)


LRU_HALO = SUBLANES


def _expm1(x):
    u = jnp.exp(x)
    return jnp.where(u == 1.0, x, (u - 1.0) * x / jnp.log(u))


def _lru_gates_kernel(x_ref, prev_ref, next_ref, cw_ref, cb_ref, wa_ref, ba_ref, wi_ref, bi_ref,
                      lam_ref, af_ref, bf_ref, ab_ref, bb_ref, *, tt, n_prompt, t_p, t_s):
    i = pl.program_id(0)
    s = i * tt
    in_prompt = s < n_prompt
    pos = jnp.where(in_prompt, s % t_p, (s - n_prompt) % t_s)
    seq_len = jnp.where(in_prompt, t_p, t_s)
    is_start = pos == 0
    is_end = pos + tt == seq_len
    x = x_ref[...]
    prev = jnp.where(is_start, 0.0, prev_ref[LRU_HALO - 2:LRU_HALO, :])
    nxt = jnp.where(is_end, 0.0, next_ref[0:1, :])
    xp = jnp.concatenate([prev, x, nxt], axis=0)
    cw = cw_ref[...]
    xc = cb_ref[...] + xp[0:tt] * cw[0:1]
    for j in range(1, CONV_W):
        xc = xc + xp[j:j + tt] * cw[j:j + 1]
    outs = ((af_ref, bf_ref), (ab_ref, bb_ref))
    for n in range(LRU_BLOCKS):
        sl = slice(n * LRU_BLOCK, (n + 1) * LRU_BLOCK)
        xb = xc[:, sl]
        xb16 = xb.astype(BF16)
        for dr in range(2):
            r = _sigmoid(jnp.dot(xb16, wa_ref[dr, n], preferred_element_type=F32) + ba_ref[dr:dr + 1, sl])
            g = _sigmoid(jnp.dot(xb16, wi_ref[dr, n], preferred_element_type=F32) + bi_ref[dr:dr + 1, sl])
            z = -lam_ref[dr:dr + 1, sl]
            softplus = jnp.maximum(z, 0.0) + jnp.log(1.0 + jnp.exp(-jnp.abs(z)))
            log_a = -LRU_C * r * softplus
            a_out, b_out = outs[dr]
            a_out[:, sl] = jnp.exp(log_a)
            b_out[:, sl] = jnp.sqrt(-_expm1(2.0 * log_a)) * g * xb


def lru_gates(proj, conv_w, conv_b, w_a, b_a, w_i, b_i, lam, *, n_prompt, t_p, t_s, tt=256):
    n = proj.shape[0]
    xcol = (3 * A_WIDTH) // LRU_WIDTH
    hb = tt // LRU_HALO
    nh = n // LRU_HALO
    full = lambda shape: pl.BlockSpec(shape, lambda i: (0,) * len(shape))
    out = jax.ShapeDtypeStruct((n, LRU_WIDTH), F32)
    ospec = pl.BlockSpec((tt, LRU_WIDTH), lambda i: (i, 0))
    return pl.pallas_call(
        functools.partial(_lru_gates_kernel, tt=tt, n_prompt=n_prompt, t_p=t_p, t_s=t_s),
        grid=(n // tt,),
        in_specs=[pl.BlockSpec((tt, LRU_WIDTH), lambda i: (i, xcol)),
                  pl.BlockSpec((LRU_HALO, LRU_WIDTH), lambda i: (jnp.maximum(i * hb - 1, 0), xcol)),
                  pl.BlockSpec((LRU_HALO, LRU_WIDTH), lambda i: (jnp.minimum((i + 1) * hb, nh - 1), xcol)),
                  full((CONV_W, LRU_WIDTH)), full((1, LRU_WIDTH)),
                  full((2, LRU_BLOCKS, LRU_BLOCK, LRU_BLOCK)), full((2, LRU_WIDTH)),
                  full((2, LRU_BLOCKS, LRU_BLOCK, LRU_BLOCK)), full((2, LRU_WIDTH)),
                  full((2, LRU_WIDTH))],
        out_specs=[ospec] * 4,
        out_shape=[out] * 4,
        compiler_params=_cparams(("arbitrary",)),
        name="lru_gates",
    )(proj, proj, proj, conv_w, conv_b.reshape(1, LRU_WIDTH), w_a.astype(BF16), b_a,
      w_i.astype(BF16), b_i, lam)


def _lru_scan_kernel(*refs, tt, reverse, combine):
    if combine:
        a_ref, b_ref, h0_ref, hf_ref, g_ref, o_ref, fin_ref, h_scr = refs
    else:
        a_ref, b_ref, h0_ref, o_ref, fin_ref, h_scr = refs
    ti = pl.program_id(1)

    @pl.when(ti == 0)
    def _():
        h_scr[...] = h0_ref[...]

    def step(j, h):
        t = tt - 1 - j if reverse else j
        h = a_ref[t] * h + b_ref[t]
        if combine:
            o_ref[t] = ((hf_ref[t] + h) * _gelu(g_ref[t])).astype(o_ref.dtype)
        else:
            o_ref[t] = h
        return h

    h = lax.fori_loop(0, tt, step, h_scr[...], unroll=8)
    h_scr[...] = h
    fin_ref[...] = h


def lru_scan(a, b, h0, tok0, nb, t, *, reverse, hf=None, proj=None, tt=256):
    tt = min(tt, t)
    nt = t // tt
    r0 = tok0 // tt
    combine = hf is not None
    vreg = (SUBLANES, LANES)
    view = lambda z: z.reshape(z.shape[0], LRU_WIDTH // LANES, LANES)

    def tmap(bi, ti):
        return nt - 1 - ti if reverse else ti

    in_specs = [pl.BlockSpec((tt,) + vreg, lambda bi, ti: (r0 + bi * nt + tmap(bi, ti), 0, 0)),
                pl.BlockSpec((tt,) + vreg, lambda bi, ti: (r0 + bi * nt + tmap(bi, ti), 0, 0)),
                pl.BlockSpec((None,) + vreg, lambda bi, ti: (bi, 0, 0))]
    args = [view(a), view(b), view(h0)]
    if combine:
        gcol = (3 * A_WIDTH + LRU_WIDTH) // LRU_WIDTH
        in_specs += [pl.BlockSpec((tt,) + vreg, lambda bi, ti: (bi * nt + tmap(bi, ti), 0, 0)),
                     pl.BlockSpec((tt,) + vreg, lambda bi, ti: (r0 + bi * nt + tmap(bi, ti), gcol, 0))]
        args += [view(hf), proj.reshape(proj.shape[0], EVEN_IN // LANES, LANES)]
    out, fin = pl.pallas_call(
        functools.partial(_lru_scan_kernel, tt=tt, reverse=reverse, combine=combine),
        grid=(nb, nt),
        in_specs=in_specs,
        out_specs=[pl.BlockSpec((tt,) + vreg, lambda bi, ti: (bi * nt + tmap(bi, ti), 0, 0)),
                   pl.BlockSpec((None,) + vreg, lambda bi, ti: (bi, 0, 0))],
        out_shape=[jax.ShapeDtypeStruct((nb * t,) + vreg, F32),
                   jax.ShapeDtypeStruct((nb,) + vreg, F32)],
        scratch_shapes=[pltpu.VMEM(vreg, F32)],
        compiler_params=_cparams(("arbitrary", "arbitrary")),
        name="lru_scan_bwd" if reverse else "lru_scan_fwd",
    )(*args)
    return out.reshape(nb * t, LRU_WIDTH), fin.reshape(nb, LRU_WIDTH)


def _topk_rows(s, k, payload=None):
    n = s.shape[0]
    iota = lax.broadcasted_iota(jnp.int32, s.shape, 0).astype(F32)
    vals, idxs = [], []
    for _ in range(k):
        m = jnp.max(s, axis=0, keepdims=True)
        am = jnp.min(jnp.where(s == m, iota, float(n)), axis=0, keepdims=True)
        sel = iota == am
        vals.append(m)
        if payload is None:
            idxs.append(am)
        else:
            idxs.append(jnp.sum(jnp.where(sel, payload, 0.0), axis=0, keepdims=True))
        s = jnp.where(sel, -jnp.inf, s)
    return jnp.concatenate(vals, axis=0), jnp.concatenate(idxs, axis=0)


def _peer_route_kernel(q_ref, sk_ref, idx_ref, gate_ref, *, tm):
    half = PEER_QDIM // 2
    for c in range(tm // LANES):
        q = q_ref[c * LANES:(c + 1) * LANES, :].astype(BF16)
        tops = []
        for p in range(2):
            s = lax.dot_general(sk_ref[p], q[:, p * half:(p + 1) * half], NT_DIMS,
                                preferred_element_type=F32)
            tops.append(_topk_rows(s, PEER_TOPK))
        (s1, i1), (s2, i2) = tops
        nb = [PEER_TOPK // (a + 1) for a in range(PEER_TOPK)]
        pad = -sum(nb) % SUBLANES
        cand = jnp.concatenate([s1[a:a + 1] + s2[:nb[a]] for a in range(PEER_TOPK)]
                               + [jnp.full((pad, LANES), -jnp.inf, F32)], axis=0)
        cidx = jnp.concatenate([i1[a:a + 1] * float(PEER_NKEYS) + i2[:nb[a]] for a in range(PEER_TOPK)]
                               + [jnp.zeros((pad, LANES), F32)], axis=0)
        bs, bidx = _topk_rows(cand, PEER_TOPK, payload=cidx)
        e = jnp.exp(bs - bs[0:1])
        gates = e / jnp.sum(e, axis=0, keepdims=True)
        idx_ref[:, c * LANES:(c + 1) * LANES] = bidx.astype(jnp.int32)
        gate_ref[:, c * LANES:(c + 1) * LANES] = gates


def peer_route(q, sub_keys, tm=256):
    n = q.shape[0]
    half = PEER_QDIM // 2
    return pl.pallas_call(
        functools.partial(_peer_route_kernel, tm=tm),
        grid=(n // tm, PEER_HEADS),
        in_specs=[pl.BlockSpec((tm, PEER_QDIM), lambda i, h: (i, h)),
                  pl.BlockSpec((None, 2, PEER_NKEYS, half), lambda i, h: (h, 0, 0, 0))],
        out_specs=[pl.BlockSpec((None, PEER_TOPK, tm), lambda i, h: (h, 0, i))] * 2,
        out_shape=[jax.ShapeDtypeStruct((PEER_HEADS, PEER_TOPK, n), jnp.int32),
                   jax.ShapeDtypeStruct((PEER_HEADS, PEER_TOPK, n), F32)],
        compiler_params=_cparams(("arbitrary", "arbitrary")),
        name="peer_route",
    )(q, sub_keys)


def _peer_gates_kernel(idx_ref, gate_ref, g_ref, g3_scr):
    idx = idx_ref[...]
    gate = gate_ref[...]
    tg, _, hk = idx.shape
    i1 = idx // PEER_NKEYS
    i2 = idx % PEER_NKEYS
    iota = lax.broadcasted_iota(jnp.int32, (tg, PEER_NKEYS, hk), 1)
    a = jnp.where(i1 == iota, gate, 0.0)
    a_hi = a.astype(BF16)
    a_lo = (a - a_hi.astype(F32)).astype(BF16)
    bsel = jnp.where(i2 == iota, 1.0, 0.0).astype(BF16)
    dims = (((2,), (2,)), ((0,), (0,)))
    g3_scr[...] = (lax.dot_general(a_hi, bsel, dims, preferred_element_type=F32)
                   + lax.dot_general(a_lo, bsel, dims, preferred_element_type=F32))
    for i in range(PEER_NKEYS):
        g_ref[:, i * PEER_NKEYS:(i + 1) * PEER_NKEYS] = g3_scr[:, i, :]


def peer_gate_rows(idx, gates, tg=64):
    n, hk = idx.shape
    return pl.pallas_call(
        _peer_gates_kernel,
        grid=(n // tg,),
        in_specs=[pl.BlockSpec((tg, 1, hk), lambda i: (i, 0, 0))] * 2,
        out_specs=pl.BlockSpec((tg, PEER_N), lambda i: (i, 0)),
        out_shape=jax.ShapeDtypeStruct((n, PEER_N), F32),
        scratch_shapes=[pltpu.VMEM((tg, PEER_NKEYS, PEER_NKEYS), F32)],
        compiler_params=_cparams(("arbitrary",)),
        name="peer_gate_rows",
    )(idx.reshape(n, 1, hk), gates.reshape(n, 1, hk))


def _peer_dense_kernel(h_ref, u_ref, v_ref, gr_ref, x_ref, mg_ref, fg_ref, o_ref, *, final_norm):
    e = pl.program_id(1)
    s = lax.dot_general(h_ref[...], u_ref[...], NT_DIMS, preferred_element_type=F32)
    act = (_gelu(s) * gr_ref[...]).astype(BF16)
    p = jnp.dot(act, v_ref[...], preferred_element_type=F32)

    @pl.when(e == 0)
    def _():
        o_ref[...] = p

    @pl.when(e > 0)
    def _():
        o_ref[...] += p

    @pl.when(e == pl.num_programs(1) - 1)
    def _():
        y = x_ref[...] + mg_ref[...] * o_ref[...]
        if final_norm:
            y = y * lax.rsqrt(jnp.mean(y * y, axis=-1, keepdims=True) + EPS) * fg_ref[...]
        o_ref[...] = y


def peer_dense(h, u16, v16, grows, x, mod, gate_idx, final_g, *, n_prompt, t_s, final_norm,
               tm=512, te=1024):
    n, d = x.shape
    ne = u16.shape[0]
    row = functools.partial(_mod_row, tm=tm, n_prompt=n_prompt, t_s=t_s)
    return pl.pallas_call(
        functools.partial(_peer_dense_kernel, final_norm=final_norm),
        grid=(n // tm, ne // te),
        in_specs=[pl.BlockSpec((tm, d), lambda i, e: (i, 0)),
                  pl.BlockSpec((te, d), lambda i, e: (e, 0)),
                  pl.BlockSpec((te, d), lambda i, e: (e, 0)),
                  pl.BlockSpec((tm, te), lambda i, e: (i, e)),
                  pl.BlockSpec((tm, d), lambda i, e: (i, 0)),
                  pl.BlockSpec((None, None, 1, d), lambda i, e: (row(i), gate_idx, 0, 0)),
                  pl.BlockSpec((1, d), lambda i, e: (0, 0))],
        out_specs=pl.BlockSpec((tm, d), lambda i, e: (i, 0)),
        out_shape=jax.ShapeDtypeStruct((n, d), F32),
        compiler_params=_cparams(("arbitrary", "arbitrary")),
        name="peer_dense",
    )(h, u16, v16, grows, x, mod, final_g.reshape(1, d))


def peer_block(x, norm_g, mod, w_q16, sk16, u16, v16, final_g, *, n_prompt, t_s, final_norm):
    n = x.shape[0]
    q, h = norm_matmul(x, norm_g, mod, 3, 4, w_q16, n_prompt=n_prompt, t_s=t_s, emit_h=True)
    idx, gates = peer_route(q, sk16)
    hk = PEER_HEADS * PEER_TOPK
    idx = idx.reshape(hk, n).T
    gates = gates.reshape(hk, n).T
    grows = peer_gate_rows(idx, gates)
    return peer_dense(h, u16, v16, grows, x, mod, 5, final_g, n_prompt=n_prompt, t_s=t_s,
                      final_norm=final_norm)


def kernel(x_prompt, x_sample, cache_a_k, cache_a_v, state_lru, cache_c_k, cache_c_v, c, c_ctx, w_mod, b_mod, norm1_g, norm2_g, final_norm_g, even_w_in, even_w_out, a_lambda, a_subln_g, lru_conv_w, lru_conv_b, lru_w_a, lru_b_a, lru_w_i, lru_b_i, lru_lambda, odd_w_in, odd_w_out, na_rpb, peer_w_q, peer_sub_keys, peer_u, peer_v):
    b_p, t_p, d = x_prompt.shape
    b_s, t_s, _ = x_sample.shape
    depth = w_mod.shape[0]
    n_p = b_p * t_p
    n_s = b_s * t_s
    lc = cache_a_k.shape[2]
    tok = dict(n_prompt=n_p, t_s=t_s)

    x = jnp.concatenate([x_prompt.reshape(n_p, d), x_sample.reshape(n_s, d)], axis=0)
    n_rows = -(-(1 + b_s) // SUBLANES) * SUBLANES
    cond = jnp.concatenate([c_ctx[None], c, jnp.zeros((n_rows - 1 - b_s, d), F32)], axis=0)
    mod_all = modulation_all(cond, w_mod, b_mod).reshape(depth, n_rows, 6, 1, d)
    rope = rope_tables(t_s)

    new_ak, new_av, new_lru, new_ck, new_cv = [], [], [], [], []
    for l in range(depth):
        mod = mod_all[l]
        if l % 2 == 0:
            e = l // 2
            lam_init = 0.8 - 0.6 * math.exp(-0.3 * l)
            proj = norm_matmul(x, norm1_g[l], mod, 0, 1, even_w_in[e].astype(BF16), **tok)
            ctx = (cache_a_k[:, e].reshape(b_s, lc, A_WIDTH), cache_a_v[:, e].reshape(b_s, lc, A_WIDTH))
            oa_p = diff_attention_call(proj, 0, b_p, t_p, a_lambda[e], a_subln_g[e], lam_init)
            oa_s = diff_attention_call(proj, n_p, b_s, t_s, a_lambda[e], a_subln_g[e], lam_init,
                                       ctx=ctx, rope=rope)
            a_f, b_f, a_b, b_b = lru_gates(proj, lru_conv_w[e], lru_conv_b[e], lru_w_a[e], lru_b_a[e],
                                           lru_w_i[e], lru_b_i[e], lru_lambda[e],
                                           n_prompt=n_p, t_p=t_p, t_s=t_s)
            zeros_p = jnp.zeros((b_p, LRU_WIDTH), F32)
            hf_p, fin_f = lru_scan(a_f, b_f, zeros_p, 0, b_p, t_p, reverse=False)
            ob_p, fin_b = lru_scan(a_b, b_b, zeros_p, 0, b_p, t_p, reverse=True, hf=hf_p, proj=proj)
            hf_s, _ = lru_scan(a_f, b_f, state_lru[:, e, 0], n_p, b_s, t_s, reverse=False)
            ob_s, _ = lru_scan(a_b, b_b, state_lru[:, e, 1], n_p, b_s, t_s, reverse=True,
                               hf=hf_s, proj=proj)
            o_a = jnp.concatenate([oa_p, oa_s], axis=0)
            o_b = jnp.concatenate([ob_p, ob_s], axis=0)
            w_out = even_w_out[e].astype(BF16)
            x = matmul_res([o_a, o_b], [w_out[:A_WIDTH], w_out[A_WIDTH:]], x, mod, 2, **tok)
            new_ak.append(proj[:n_p, A_WIDTH:2 * A_WIDTH].reshape(b_p, t_p, A_HEADS, 2 * A_QK_DIM))
            new_av.append(proj[:n_p, 2 * A_WIDTH:3 * A_WIDTH].reshape(b_p, t_p, A_HEADS, A_V_DIM))
            new_lru.append(jnp.stack([fin_f, fin_b], axis=1))
        else:
            o = l // 2
            proj = norm_matmul(x, norm1_g[l], mod, 0, 1, odd_w_in[o].astype(BF16), **tok)
            oc_p = attention_call(proj, b_p, t_p)
            bias = na_bias_tables(na_rpb[o], t_s // GRID_W)
            oc_s = na_call(proj, n_p, b_s, t_s, cache_c_k[:, o].reshape(b_s, lc, C_WIDTH),
                           cache_c_v[:, o].reshape(b_s, lc, C_WIDTH), bias)
            o_c = jnp.concatenate([oc_p, oc_s], axis=0)
            x = matmul_res([o_c], [odd_w_out[o].astype(BF16)], x, mod, 2, **tok)
            new_ck.append(proj[:n_p, C_WIDTH:2 * C_WIDTH].reshape(b_p, t_p, C_HEADS, C_HEAD_DIM))
            new_cv.append(proj[:n_p, 2 * C_WIDTH:3 * C_WIDTH].reshape(b_p, t_p, C_HEADS, C_HEAD_DIM))
        x = peer_block(x, norm2_g[l], mod, peer_w_q[l].astype(BF16), peer_sub_keys[l].astype(BF16),
                       peer_u[l].astype(BF16), peer_v[l].astype(BF16), final_norm_g,
                       final_norm=(l == depth - 1), **tok)
    y_prompt = x[:n_p].reshape(b_p, t_p, d)
    y_sample = x[n_p:].reshape(b_s, t_s, d)
    return (y_prompt, y_sample, jnp.stack(new_ak, axis=1), jnp.stack(new_av, axis=1),
            jnp.stack(new_lru, axis=1), jnp.stack(new_ck, axis=1), jnp.stack(new_cv, axis=1))
```

```python
import functools
import math

import numpy as np
import jax
import jax.numpy as jnp
from jax import lax
from jax.experimental import pallas as pl
from jax.experimental.pallas import tpu as pltpu

F32 = jnp.float32
BF16 = jnp.bfloat16

D_MODEL = 2048
GRID_W = 64
EPS = 1e-6
ROPE_BASE = 10000.0
A_HEADS = 8
A_QK_DIM = 64
A_V_DIM = 128
A_WIDTH = A_HEADS * A_V_DIM
LRU_WIDTH = 1024
LRU_BLOCKS = 8
LRU_BLOCK = LRU_WIDTH // LRU_BLOCKS
CONV_W = 4
LRU_C = 8.0
EVEN_IN = 3 * A_WIDTH + 2 * LRU_WIDTH
C_HEADS = 16
C_HEAD_DIM = 128
C_WIDTH = C_HEADS * C_HEAD_DIM
NA_KH = 8
NA_KW = 16
PEER_HEADS = 8
PEER_NKEYS = 128
PEER_N = PEER_NKEYS * PEER_NKEYS
PEER_QDIM = 256
PEER_TOPK = 16

LANES = 128
SUBLANES = 8
VMEM_LIMIT = 56 * 1024 * 1024

NT_DIMS = (((1,), (1,)), ((), ()))


def _cparams(sem):
    return pltpu.CompilerParams(dimension_semantics=sem, vmem_limit_bytes=VMEM_LIMIT)


def _gelu(x):
    c = math.sqrt(2.0 / math.pi)
    return 0.5 * x * (1.0 + jnp.tanh(c * (x + 0.044715 * (x * x * x))))


def _sigmoid(x):
    return 1.0 / (1.0 + jnp.exp(-x))


def _mod_kernel(c_ref, w_ref, b_ref, o_ref):
    c = c_ref[...]
    s = c * _sigmoid(c)
    o_ref[0] = jnp.dot(s.astype(BF16), w_ref[0].astype(BF16),
                       preferred_element_type=F32) + b_ref[0]


def modulation_all(cond, w_mod, b_mod):
    depth, d, n6 = w_mod.shape
    r = cond.shape[0]
    tn = 768
    return pl.pallas_call(
        _mod_kernel,
        grid=(depth, n6 // tn),
        in_specs=[pl.BlockSpec((r, d), lambda l, j: (0, 0)),
                  pl.BlockSpec((1, d, tn), lambda l, j: (l, 0, j)),
                  pl.BlockSpec((1, 1, tn), lambda l, j: (l, 0, j))],
        out_specs=pl.BlockSpec((1, r, tn), lambda l, j: (l, 0, j)),
        out_shape=jax.ShapeDtypeStruct((depth, r, n6), F32),
        compiler_params=_cparams(("arbitrary", "arbitrary")),
        name="modulation",
    )(cond, w_mod, b_mod.reshape(depth, 1, n6))


def _mod_row(i, tm, n_prompt, t_s):
    s = i * tm
    return jnp.where(s < n_prompt, 0, 1 + (s - n_prompt) // t_s)


def _norm_matmul_kernel(x_ref, g_ref, sh_ref, sc_ref, w_ref, o_ref, *rest, emit_h):
    if emit_h:
        h_out_ref, h_scr = rest
    else:
        (h_scr,) = rest

    @pl.when(pl.program_id(1) == 0)
    def _():
        x = x_ref[...]
        y = x * lax.rsqrt(jnp.mean(x * x, axis=-1, keepdims=True) + EPS)
        h = (y * g_ref[...]) * (1.0 + sc_ref[...]) + sh_ref[...]
        h_scr[...] = h.astype(BF16)
        if emit_h:
            h_out_ref[...] = h.astype(BF16)

    o_ref[...] = jnp.dot(h_scr[...], w_ref[...], preferred_element_type=F32).astype(o_ref.dtype)


def norm_matmul(x, g, mod, shift_idx, scale_idx, w, *, n_prompt, t_s, emit_h=False,
                tm=1024, tn=512, out_dtype=F32):
    n, d = x.shape
    nout = w.shape[1]
    row = functools.partial(_mod_row, tm=tm, n_prompt=n_prompt, t_s=t_s)
    out_shape = [jax.ShapeDtypeStruct((n, nout), out_dtype)]
    out_specs = [pl.BlockSpec((tm, tn), lambda i, j: (i, j))]
    if emit_h:
        out_shape.append(jax.ShapeDtypeStruct((n, d), BF16))
        out_specs.append(pl.BlockSpec((tm, d), lambda i, j: (i, 0)))
    res = pl.pallas_call(
        functools.partial(_norm_matmul_kernel, emit_h=emit_h),
        grid=(n // tm, nout // tn),
        in_specs=[pl.BlockSpec((tm, d), lambda i, j: (i, 0)),
                  pl.BlockSpec((1, d), lambda i, j: (0, 0)),
                  pl.BlockSpec((None, None, 1, d), lambda i, j: (row(i), shift_idx, 0, 0)),
                  pl.BlockSpec((None, None, 1, d), lambda i, j: (row(i), scale_idx, 0, 0)),
                  pl.BlockSpec((d, tn), lambda i, j: (0, j))],
        out_specs=out_specs,
        out_shape=out_shape,
        scratch_shapes=[pltpu.VMEM((tm, d), BF16)],
        compiler_params=_cparams(("arbitrary", "arbitrary")),
        name="norm_matmul",
    )(x, g.reshape(1, d), mod, mod, w)
    return res if emit_h else res[0]


def _matmul_res_kernel(*refs, n_in):
    a_refs = refs[:n_in]
    w_refs = refs[n_in:2 * n_in]
    x_ref, gate_ref, o_ref = refs[2 * n_in:]
    acc = None
    for a_ref, w_ref in zip(a_refs, w_refs):
        p = jnp.dot(a_ref[...].astype(BF16), w_ref[...], preferred_element_type=F32)
        acc = p if acc is None else acc + p
    o_ref[...] = x_ref[...] + gate_ref[...] * acc


def matmul_res(a_list, w_list, x, mod, gate_idx, *, n_prompt, t_s, tm=512, tn=512):
    n, d = x.shape
    n_in = len(a_list)
    row = functools.partial(_mod_row, tm=tm, n_prompt=n_prompt, t_s=t_s)
    in_specs = ([pl.BlockSpec((tm, a.shape[1]), lambda i, j: (i, 0)) for a in a_list]
                + [pl.BlockSpec((w.shape[0], tn), lambda i, j: (0, j)) for w in w_list]
                + [pl.BlockSpec((tm, tn), lambda i, j: (i, j)),
                   pl.BlockSpec((None, None, 1, tn), lambda i, j: (row(i), gate_idx, 0, j))])
    return pl.pallas_call(
        functools.partial(_matmul_res_kernel, n_in=n_in),
        grid=(n // tm, d // tn),
        in_specs=in_specs,
        out_specs=pl.BlockSpec((tm, tn), lambda i, j: (i, j)),
        out_shape=jax.ShapeDtypeStruct((n, d), F32),
        compiler_params=_cparams(("arbitrary", "arbitrary")),
        name="matmul_res",
    )(*a_list, *w_list, x, mod)


def _rope(x, cos, sin_signed):
    lane = lax.broadcasted_iota(jnp.int32, x.shape, 1)
    first = (lane % 32) < 16
    partner = jnp.where(first, pltpu.roll(x, LANES - 16, axis=1), pltpu.roll(x, 16, axis=1))
    return x * cos + partner * sin_signed


def _diff_attn_kernel(*refs, has_ctx, lam_init, tq, n_sub):
    if has_ctx:
        (q_ref, k_ref, v_ref, ck_ref, cv_ref, cos_ref, sin_ref, lam_ref, g_ref,
         o_ref, k_scr, v_scr, ck_scr, cv_scr) = refs
    else:
        q_ref, k_ref, v_ref, lam_ref, g_ref, o_ref, k_scr, v_scr = refs
    qi = pl.program_id(2)

    @pl.when(qi == 0)
    def _():
        k = k_ref[...]
        if has_ctx:
            k = _rope(k, cos_ref[...], sin_ref[...])
            ck_scr[...] = ck_ref[...].astype(BF16)
            cv_scr[...] = cv_ref[...].astype(BF16)
        k_scr[...] = k.astype(BF16)
        v_scr[...] = v_ref[...].astype(BF16)

    q = q_ref[...]
    if has_ctx:
        off = pl.multiple_of(qi * tq, tq)
        q = _rope(q, cos_ref[pl.ds(off, tq), :], sin_ref[pl.ds(off, tq), :])
    q = q * (A_QK_DIM ** -0.5)
    lp = lam_ref[...]
    lam = (jnp.exp(jnp.sum(lp[0:1] * lp[1:2], axis=-1, keepdims=True))
           - jnp.exp(jnp.sum(lp[2:3] * lp[3:4], axis=-1, keepdims=True)) + lam_init)
    ts = tq // n_sub
    lane = lax.broadcasted_iota(jnp.int32, (ts, LANES), 1)

    scored = []
    for a in range(n_sub):
        qa = q[a * ts:(a + 1) * ts]
        q2 = jnp.concatenate([jnp.where(lane < A_QK_DIM, qa, 0.0),
                              jnp.where(lane >= A_QK_DIM, qa, 0.0)], axis=0).astype(BF16)
        s = lax.dot_general(q2, k_scr[...], NT_DIMS, preferred_element_type=F32)
        mx = jnp.max(s, axis=-1, keepdims=True)
        sc = None
        if has_ctx:
            sc = lax.dot_general(q2, ck_scr[...], NT_DIMS, preferred_element_type=F32)
            mx = jnp.maximum(mx, jnp.max(sc, axis=-1, keepdims=True))
        scored.append((s, sc, mx))
    for a, (s, sc, mx) in enumerate(scored):
        e = jnp.exp(s - mx)
        l = jnp.sum(e, axis=-1, keepdims=True)
        o = jnp.dot(e.astype(BF16), v_scr[...], preferred_element_type=F32)
        if has_ctx:
            ec = jnp.exp(sc - mx)
            l = l + jnp.sum(ec, axis=-1, keepdims=True)
            o = o + jnp.dot(ec.astype(BF16), cv_scr[...], preferred_element_type=F32)
        o = o / l
        d = o[:ts] - lam * o[ts:]
        y = d * lax.rsqrt(jnp.mean(d * d, axis=-1, keepdims=True) + EPS)
        o_ref[a * ts:(a + 1) * ts, :] = ((y * g_ref[...]) * (1.0 - lam_init)).astype(o_ref.dtype)


def diff_attention_call(proj, tok0, b, t, lam_p, subln_g, lam_init, ctx=None, rope=None, tq=256,
                        n_sub=2):
    tq = min(tq, t)
    nq = t // tq
    q0 = tok0 // tq
    k0 = tok0 // t
    has_ctx = ctx is not None
    in_specs = [pl.BlockSpec((tq, LANES), lambda bi, h, qi: (q0 + bi * nq + qi, h)),
                pl.BlockSpec((t, LANES), lambda bi, h, qi: (k0 + bi, A_HEADS + h)),
                pl.BlockSpec((t, LANES), lambda bi, h, qi: (k0 + bi, 2 * A_HEADS + h))]
    args = [proj, proj, proj]
    if has_ctx:
        ck, cv = ctx
        lc = ck.shape[1]
        cos, sin = rope
        in_specs += [pl.BlockSpec((None, lc, LANES), lambda bi, h, qi: (bi, 0, h)),
                     pl.BlockSpec((None, lc, LANES), lambda bi, h, qi: (bi, 0, h)),
                     pl.BlockSpec((t, LANES), lambda bi, h, qi: (0, 0)),
                     pl.BlockSpec((t, LANES), lambda bi, h, qi: (0, 0))]
        args += [ck, cv, cos, sin]
    in_specs += [pl.BlockSpec((4, A_QK_DIM), lambda bi, h, qi: (0, 0)),
                 pl.BlockSpec((1, A_V_DIM), lambda bi, h, qi: (0, 0))]
    args += [lam_p, subln_g.reshape(1, A_V_DIM)]
    scratch = [pltpu.VMEM((t, LANES), BF16), pltpu.VMEM((t, LANES), BF16)]
    if has_ctx:
        scratch += [pltpu.VMEM((lc, LANES), BF16), pltpu.VMEM((lc, LANES), BF16)]
    return pl.pallas_call(
        functools.partial(_diff_attn_kernel, has_ctx=has_ctx, lam_init=lam_init, tq=tq, n_sub=n_sub),
        grid=(b, A_HEADS, nq),
        in_specs=in_specs,
        out_specs=pl.BlockSpec((tq, LANES), lambda bi, h, qi: (bi * nq + qi, h)),
        out_shape=jax.ShapeDtypeStruct((b * t, A_WIDTH), BF16),
        scratch_shapes=scratch,
        compiler_params=_cparams(("arbitrary", "arbitrary", "arbitrary")),
        name="diff_attention",
    )(*args)


def rope_tables(t):
    tok = jnp.arange(t)
    row = (tok // GRID_W).astype(F32)
    col = (tok % GRID_W).astype(F32)
    n_freq = A_QK_DIM // 4
    inv = ROPE_BASE ** (-jnp.arange(n_freq, dtype=F32) / n_freq)
    ang_r = row[:, None] * inv
    ang_c = col[:, None] * inv
    cos64 = jnp.concatenate([jnp.cos(ang_r), jnp.cos(ang_r), jnp.cos(ang_c), jnp.cos(ang_c)], axis=1)
    sin64 = jnp.concatenate([-jnp.sin(ang_r), jnp.sin(ang_r), -jnp.sin(ang_c), jnp.sin(ang_c)], axis=1)
    return jnp.tile(cos64, (1, 2)), jnp.tile(sin64, (1, 2))


def _attn_kernel(q_ref, k_ref, v_ref, o_ref):
    q = q_ref[...].astype(BF16)
    k = k_ref[...].astype(BF16)
    s = lax.dot_general(q, k, NT_DIMS, preferred_element_type=F32) * (C_HEAD_DIM ** -0.5)
    mx = jnp.max(s, axis=-1, keepdims=True)
    e = jnp.exp(s - mx)
    l = jnp.sum(e, axis=-1, keepdims=True)
    o = jnp.dot(e.astype(BF16), v_ref[...].astype(BF16), preferred_element_type=F32)
    o_ref[...] = (o / l).astype(o_ref.dtype)


def attention_call(proj, b, t):
    return pl.pallas_call(
        _attn_kernel,
        grid=(b, C_HEADS),
        in_specs=[pl.BlockSpec((t, LANES), lambda bi, h: (bi, h)),
                  pl.BlockSpec((t, LANES), lambda bi, h: (bi, C_HEADS + h)),
                  pl.BlockSpec((t, LANES), lambda bi, h: (bi, 2 * C_HEADS + h))],
        out_specs=pl.BlockSpec((t, LANES), lambda bi, h: (bi, h)),
        out_shape=jax.ShapeDtypeStruct((b * t, C_WIDTH), BF16),
        compiler_params=_cparams(("arbitrary", "arbitrary")),
        name="ctx_attention",
    )(proj, proj, proj)


NA_QROWS = 8
NA_WROWS = 16
NA_CHUNK_ROWS = 4
NA_NCHUNK = NA_WROWS // NA_CHUNK_ROWS
NA_NSUB = 2


def _na_window_start(rb, rows):
    return np.clip(rb * NA_QROWS - NA_KH // 2, 0, rows - NA_WROWS)


def na_bias_tables(rpb, rows):
    n_rb = rows // NA_QROWS
    geos = []
    for rb in (0, 1, n_rb - 1):
        r0 = rb * NA_QROWS
        ws = _na_window_start(rb, rows)
        qr = r0 + np.arange(NA_QROWS)[:, None, None, None]
        qc = np.arange(GRID_W)[None, :, None, None]
        kr = ws + np.arange(NA_WROWS)[None, None, :, None]
        kc = np.arange(GRID_W)[None, None, None, :]
        rs = np.clip(qr - NA_KH // 2, 0, rows - NA_KH)
        cs = np.clip(qc - NA_KW // 2, 0, GRID_W - NA_KW)
        ok = (kr >= rs) & (kr < rs + NA_KH) & (kc >= cs) & (kc < cs + NA_KW)
        ri = np.clip(kr - qr + NA_KH - 1, 0, 2 * NA_KH - 2)
        ci = np.clip(kc - qc + NA_KW - 1, 0, 2 * NA_KW - 2)
        shape = (NA_QROWS, GRID_W, NA_WROWS, GRID_W)
        ok = np.broadcast_to(ok, shape).reshape(NA_QROWS * GRID_W, NA_WROWS * GRID_W)
        by_col = rpb[:, :, ci[0, :, 0, :]]
        tab = by_col[:, ri[:, 0, :, 0]]
        tab = tab.transpose(0, 1, 3, 2, 4).reshape((rpb.shape[0],) + ok.shape)
        geos.append(jnp.where(ok[None], tab, -jnp.inf))
    return jnp.stack(geos, axis=1).astype(F32)


def _na_kernel(*refs):
    q_ref = refs[0]
    k_refs = refs[1:1 + NA_NCHUNK]
    v_refs = refs[1 + NA_NCHUNK:1 + 2 * NA_NCHUNK]
    ck_ref, cv_ref, bias_ref, o_ref = refs[1 + 2 * NA_NCHUNK:]
    scale = C_HEAD_DIM ** -0.5
    k = jnp.concatenate([r[...].astype(BF16) for r in k_refs], axis=0)
    v = jnp.concatenate([r[...].astype(BF16) for r in v_refs], axis=0)
    ck = ck_ref[...].astype(BF16)
    cv = cv_ref[...].astype(BF16)
    ts = q_ref.shape[0] // NA_NSUB
    scored = []
    for a in range(NA_NSUB):
        rows = slice(a * ts, (a + 1) * ts)
        q = q_ref[rows, :].astype(BF16)
        s = lax.dot_general(q, k, NT_DIMS, preferred_element_type=F32) * scale + bias_ref[rows, :]
        sc = lax.dot_general(q, ck, NT_DIMS, preferred_element_type=F32) * scale
        mx = jnp.maximum(jnp.max(s, axis=-1, keepdims=True), jnp.max(sc, axis=-1, keepdims=True))
        scored.append((s, sc, mx))
    for a, (s, sc, mx) in enumerate(scored):
        e = jnp.exp(s - mx)
        ec = jnp.exp(sc - mx)
        l = jnp.sum(e, axis=-1, keepdims=True) + jnp.sum(ec, axis=-1, keepdims=True)
        o = (jnp.dot(e.astype(BF16), v, preferred_element_type=F32)
             + jnp.dot(ec.astype(BF16), cv, preferred_element_type=F32))
        o_ref[a * ts:(a + 1) * ts, :] = (o / l).astype(o_ref.dtype)


def na_call(proj, tok0, b, t, ck, cv, bias):
    rows = t // GRID_W
    n_rb = rows // NA_QROWS
    qtok = NA_QROWS * GRID_W
    ctok = NA_CHUNK_ROWS * GRID_W
    lc = ck.shape[1]
    q0 = tok0 // qtok
    c0 = tok0 // ctok
    cps = t // ctok

    def wchunk(rb):
        return jnp.clip(rb * (NA_QROWS // NA_CHUNK_ROWS) - 1, 0, (rows - NA_WROWS) // NA_CHUNK_ROWS)

    def geo(rb):
        return jnp.where(rb == 0, 0, jnp.where(rb == n_rb - 1, 2, 1))

    def kspec(c, colblk):
        return pl.BlockSpec((ctok, LANES),
                            lambda h, rb, bi: (c0 + bi * cps + wchunk(rb) + c, colblk + h))

    in_specs = ([pl.BlockSpec((qtok, LANES), lambda h, rb, bi: (q0 + bi * n_rb + rb, h))]
                + [kspec(c, C_HEADS) for c in range(NA_NCHUNK)]
                + [kspec(c, 2 * C_HEADS) for c in range(NA_NCHUNK)]
                + [pl.BlockSpec((None, lc, LANES), lambda h, rb, bi: (bi, 0, h)),
                   pl.BlockSpec((None, lc, LANES), lambda h, rb, bi: (bi, 0, h)),
                   pl.BlockSpec((None, None, qtok, NA_WROWS * GRID_W),
                                lambda h, rb, bi: (h, geo(rb), 0, 0))])
    return pl.pallas_call(
        _na_kernel,
        grid=(C_HEADS, n_rb, b),
        in_specs=in_specs,
        out_specs=pl.BlockSpec((qtok, LANES), lambda h, rb, bi: (bi * n_rb + rb, h)),
        out_shape=jax.ShapeDtypeStruct((b * t, C_WIDTH), BF16),
        compiler_params=_cparams(("arbitrary", "arbitrary", "arbitrary")),
        name="neighbourhood_attention",
    )(*([proj] * (1 + 2 * NA_NCHUNK)), ck, cv, bias)


LRU_HALO = SUBLANES


def _expm1(x):
    u = jnp.exp(x)
    return jnp.where(u == 1.0, x, (u - 1.0) * x / jnp.log(u))


def _lru_gates_kernel(x_ref, prev_ref, next_ref, cw_ref, cb_ref, wa_ref, ba_ref, wi_ref, bi_ref,
                      lam_ref, af_ref, bf_ref, ab_ref, bb_ref, *, tt, n_prompt, t_p, t_s):
    i = pl.program_id(0)
    s = i * tt
    in_prompt = s < n_prompt
    pos = jnp.where(in_prompt, s % t_p, (s - n_prompt) % t_s)
    seq_len = jnp.where(in_prompt, t_p, t_s)
    is_start = pos == 0
    is_end = pos + tt == seq_len
    x = x_ref[...]
    prev = jnp.where(is_start, 0.0, prev_ref[LRU_HALO - 2:LRU_HALO, :])
    nxt = jnp.where(is_end, 0.0, next_ref[0:1, :])
    xp = jnp.concatenate([prev, x, nxt], axis=0)
    cw = cw_ref[...]
    xc = cb_ref[...] + xp[0:tt] * cw[0:1]
    for j in range(1, CONV_W):
        xc = xc + xp[j:j + tt] * cw[j:j + 1]
    outs = ((af_ref, bf_ref), (ab_ref, bb_ref))
    for n in range(LRU_BLOCKS):
        sl = slice(n * LRU_BLOCK, (n + 1) * LRU_BLOCK)
        xb = xc[:, sl]
        xb16 = xb.astype(BF16)
        for dr in range(2):
            r = _sigmoid(jnp.dot(xb16, wa_ref[dr, n], preferred_element_type=F32) + ba_ref[dr:dr + 1, sl])
            g = _sigmoid(jnp.dot(xb16, wi_ref[dr, n], preferred_element_type=F32) + bi_ref[dr:dr + 1, sl])
            z = -lam_ref[dr:dr + 1, sl]
            softplus = jnp.maximum(z, 0.0) + jnp.log(1.0 + jnp.exp(-jnp.abs(z)))
            log_a = -LRU_C * r * softplus
            a_out, b_out = outs[dr]
            a_out[:, sl] = jnp.exp(log_a)
            b_out[:, sl] = jnp.sqrt(-_expm1(2.0 * log_a)) * g * xb


def lru_gates(proj, conv_w, conv_b, w_a, b_a, w_i, b_i, lam, *, n_prompt, t_p, t_s, tt=256):
    n = proj.shape[0]
    xcol = (3 * A_WIDTH) // LRU_WIDTH
    hb = tt // LRU_HALO
    nh = n // LRU_HALO
    full = lambda shape: pl.BlockSpec(shape, lambda i: (0,) * len(shape))
    out = jax.ShapeDtypeStruct((n, LRU_WIDTH), F32)
    ospec = pl.BlockSpec((tt, LRU_WIDTH), lambda i: (i, 0))
    return pl.pallas_call(
        functools.partial(_lru_gates_kernel, tt=tt, n_prompt=n_prompt, t_p=t_p, t_s=t_s),
        grid=(n // tt,),
        in_specs=[pl.BlockSpec((tt, LRU_WIDTH), lambda i: (i, xcol)),
                  pl.BlockSpec((LRU_HALO, LRU_WIDTH), lambda i: (jnp.maximum(i * hb - 1, 0), xcol)),
                  pl.BlockSpec((LRU_HALO, LRU_WIDTH), lambda i: (jnp.minimum((i + 1) * hb, nh - 1), xcol)),
                  full((CONV_W, LRU_WIDTH)), full((1, LRU_WIDTH)),
                  full((2, LRU_BLOCKS, LRU_BLOCK, LRU_BLOCK)), full((2, LRU_WIDTH)),
                  full((2, LRU_BLOCKS, LRU_BLOCK, LRU_BLOCK)), full((2, LRU_WIDTH)),
                  full((2, LRU_WIDTH))],
        out_specs=[ospec] * 4,
        out_shape=[out] * 4,
        compiler_params=_cparams(("arbitrary",)),
        name="lru_gates",
    )(proj, proj, proj, conv_w, conv_b.reshape(1, LRU_WIDTH), w_a.astype(BF16), b_a,
      w_i.astype(BF16), b_i, lam)


def _lru_scan_kernel(*refs, tt, reverse, combine):
    if combine:
        a_ref, b_ref, h0_ref, hf_ref, g_ref, o_ref, fin_ref, h_scr = refs
    else:
        a_ref, b_ref, h0_ref, o_ref, fin_ref, h_scr = refs
    ti = pl.program_id(1)

    @pl.when(ti == 0)
    def _():
        h_scr[...] = h0_ref[...]

    def step(j, h):
        t = tt - 1 - j if reverse else j
        h = a_ref[t] * h + b_ref[t]
        if combine:
            o_ref[t] = ((hf_ref[t] + h) * _gelu(g_ref[t])).astype(o_ref.dtype)
        else:
            o_ref[t] = h
        return h

    h = lax.fori_loop(0, tt, step, h_scr[...], unroll=8)
    h_scr[...] = h
    fin_ref[...] = h


def lru_scan(a, b, h0, tok0, nb, t, *, reverse, hf=None, proj=None, tt=256):
    tt = min(tt, t)
    nt = t // tt
    r0 = tok0 // tt
    combine = hf is not None
    vreg = (SUBLANES, LANES)
    view = lambda z: z.reshape(z.shape[0], LRU_WIDTH // LANES, LANES)

    def tmap(bi, ti):
        return nt - 1 - ti if reverse else ti

    in_specs = [pl.BlockSpec((tt,) + vreg, lambda bi, ti: (r0 + bi * nt + tmap(bi, ti), 0, 0)),
                pl.BlockSpec((tt,) + vreg, lambda bi, ti: (r0 + bi * nt + tmap(bi, ti), 0, 0)),
                pl.BlockSpec((None,) + vreg, lambda bi, ti: (bi, 0, 0))]
    args = [view(a), view(b), view(h0)]
    if combine:
        gcol = (3 * A_WIDTH + LRU_WIDTH) // LRU_WIDTH
        in_specs += [pl.BlockSpec((tt,) + vreg, lambda bi, ti: (bi * nt + tmap(bi, ti), 0, 0)),
                     pl.BlockSpec((tt,) + vreg, lambda bi, ti: (r0 + bi * nt + tmap(bi, ti), gcol, 0))]
        args += [view(hf), proj.reshape(proj.shape[0], EVEN_IN // LANES, LANES)]
    out, fin = pl.pallas_call(
        functools.partial(_lru_scan_kernel, tt=tt, reverse=reverse, combine=combine),
        grid=(nb, nt),
        in_specs=in_specs,
        out_specs=[pl.BlockSpec((tt,) + vreg, lambda bi, ti: (bi * nt + tmap(bi, ti), 0, 0)),
                   pl.BlockSpec((None,) + vreg, lambda bi, ti: (bi, 0, 0))],
        out_shape=[jax.ShapeDtypeStruct((nb * t,) + vreg, F32),
                   jax.ShapeDtypeStruct((nb,) + vreg, F32)],
        scratch_shapes=[pltpu.VMEM(vreg, F32)],
        compiler_params=_cparams(("arbitrary", "arbitrary")),
        name="lru_scan_bwd" if reverse else "lru_scan_fwd",
    )(*args)
    return out.reshape(nb * t, LRU_WIDTH), fin.reshape(nb, LRU_WIDTH)


def _topk_rows(s_list, k, payloads=None):
    n = s_list[0].shape[0]
    iota = lax.broadcasted_iota(jnp.int32, s_list[0].shape, 0).astype(F32)
    s_list = list(s_list)
    vals = [[] for _ in s_list]
    idxs = [[] for _ in s_list]
    for _ in range(k):
        for j, s in enumerate(s_list):
            m = jnp.max(s, axis=0, keepdims=True)
            am = jnp.min(jnp.where(s == m, iota, float(n)), axis=0, keepdims=True)
            sel = iota == am
            vals[j].append(m)
            if payloads is None:
                idxs[j].append(am)
            else:
                idxs[j].append(jnp.sum(jnp.where(sel, payloads[j], 0.0), axis=0, keepdims=True))
            s_list[j] = jnp.where(sel, -jnp.inf, s)
    return [(jnp.concatenate(v, axis=0), jnp.concatenate(i, axis=0)) for v, i in zip(vals, idxs)]


def _peer_route_kernel(q_ref, sk_ref, idx_ref, gate_ref, *, tm):
    half = PEER_QDIM // 2
    n_chunk = tm // LANES
    scores = []
    for c in range(n_chunk):
        q = q_ref[c * LANES:(c + 1) * LANES, :].astype(BF16)
        for p in range(2):
            scores.append(lax.dot_general(sk_ref[p], q[:, p * half:(p + 1) * half], NT_DIMS,
                                          preferred_element_type=F32))
    tops = _topk_rows(scores, PEER_TOPK)
    nb = [PEER_TOPK // (a + 1) for a in range(PEER_TOPK)]
    pad = -sum(nb) % SUBLANES
    cands, cidxs = [], []
    for c in range(n_chunk):
        (s1, i1), (s2, i2) = tops[2 * c], tops[2 * c + 1]
        cands.append(jnp.concatenate([s1[a:a + 1] + s2[:nb[a]] for a in range(PEER_TOPK)]
                                     + [jnp.full((pad, LANES), -jnp.inf, F32)], axis=0))
        cidxs.append(jnp.concatenate(
            [i1[a:a + 1] * float(PEER_NKEYS) + i2[:nb[a]] for a in range(PEER_TOPK)]
            + [jnp.zeros((pad, LANES), F32)], axis=0))
    best = _topk_rows(cands, PEER_TOPK, payloads=cidxs)
    for c, (bs, bidx) in enumerate(best):
        e = jnp.exp(bs - bs[0:1])
        gates = e / jnp.sum(e, axis=0, keepdims=True)
        idx_ref[:, c * LANES:(c + 1) * LANES] = bidx.astype(jnp.int32)
        gate_ref[:, c * LANES:(c + 1) * LANES] = gates


def peer_route(q, sub_keys, tm=256):
    n = q.shape[0]
    half = PEER_QDIM // 2
    return pl.pallas_call(
        functools.partial(_peer_route_kernel, tm=tm),
        grid=(n // tm, PEER_HEADS),
        in_specs=[pl.BlockSpec((tm, PEER_QDIM), lambda i, h: (i, h)),
                  pl.BlockSpec((None, 2, PEER_NKEYS, half), lambda i, h: (h, 0, 0, 0))],
        out_specs=[pl.BlockSpec((None, PEER_TOPK, tm), lambda i, h: (h, 0, i))] * 2,
        out_shape=[jax.ShapeDtypeStruct((PEER_HEADS, PEER_TOPK, n), jnp.int32),
                   jax.ShapeDtypeStruct((PEER_HEADS, PEER_TOPK, n), F32)],
        compiler_params=_cparams(("arbitrary", "arbitrary")),
        name="peer_route",
    )(q, sub_keys)


def _peer_gates_kernel(idx_ref, gate_ref, g_ref):
    idx = idx_ref[...]
    gate = gate_ref[...]
    tg, _, hk = idx.shape
    i1 = (idx // PEER_NKEYS).astype(F32).astype(BF16)
    i2 = (idx % PEER_NKEYS).astype(F32).astype(BF16)
    g_hi = gate.astype(BF16)
    g_lo = (gate - g_hi.astype(F32)).astype(BF16)
    iota = lax.broadcasted_iota(jnp.int32, (1, PEER_NKEYS, hk), 1).astype(F32).astype(BF16)
    zero = jnp.zeros((), BF16)
    m1 = i1 == iota
    a_hi = jnp.where(m1, g_hi, zero)
    a_lo = jnp.where(m1, g_lo, zero)
    bsel = jnp.where(i2 == iota, jnp.ones((), BF16), zero)
    dims = (((2,), (2,)), ((0,), (0,)))
    g3 = lax.dot_general(jnp.concatenate([a_hi, a_lo], axis=2), jnp.concatenate([bsel, bsel], axis=2),
                         dims, preferred_element_type=F32)
    g_ref[...] = pltpu.einshape("nij->inj", g3)


def peer_gate_rows(idx, gates, tg=64):
    n, hk = idx.shape
    return pl.pallas_call(
        _peer_gates_kernel,
        grid=(n // tg,),
        in_specs=[pl.BlockSpec((tg, 1, hk), lambda i: (i, 0, 0))] * 2,
        out_specs=pl.BlockSpec((PEER_NKEYS, tg, PEER_NKEYS), lambda i: (0, i, 0)),
        out_shape=jax.ShapeDtypeStruct((PEER_NKEYS, n, PEER_NKEYS), F32),
        compiler_params=_cparams(("arbitrary",)),
        name="peer_gate_rows",
    )(idx.reshape(n, 1, hk), gates.reshape(n, 1, hk))


def _peer_dense_kernel(h_ref, u_ref, v_ref, gr_ref, x_ref, mg_ref, fg_ref, o_ref, *, final_norm,
                       row_split):
    e = pl.program_id(1)

    @pl.when(e == 0)
    def _():
        o_ref[...] = jnp.zeros_like(o_ref)

    tr = h_ref.shape[0] // row_split
    scores = [lax.dot_general(h_ref[r * tr:(r + 1) * tr, :], u_ref[...], NT_DIMS,
                              preferred_element_type=F32) for r in range(row_split)]
    for r, s in enumerate(scores):
        rows = slice(r * tr, (r + 1) * tr)
        act = jnp.concatenate(
            [(_gelu(s[:, j * PEER_NKEYS:(j + 1) * PEER_NKEYS]) * gr_ref[j, rows, :]).astype(BF16)
             for j in range(gr_ref.shape[0])], axis=1)
        o_ref[rows, :] += jnp.dot(act, v_ref[...], preferred_element_type=F32)

    @pl.when(e == pl.num_programs(1) - 1)
    def _():
        y = x_ref[...] + mg_ref[...] * o_ref[...]
        if final_norm:
            y = y * lax.rsqrt(jnp.mean(y * y, axis=-1, keepdims=True) + EPS) * fg_ref[...]
        o_ref[...] = y


def peer_dense(h, u16, v16, grows, x, mod, gate_idx, final_g, *, n_prompt, t_s, final_norm,
               tm=512, te=1024, row_split=2):
    n, d = x.shape
    ne = u16.shape[0]
    row = functools.partial(_mod_row, tm=tm, n_prompt=n_prompt, t_s=t_s)
    return pl.pallas_call(
        functools.partial(_peer_dense_kernel, final_norm=final_norm, row_split=row_split),
        grid=(n // tm, ne // te),
        in_specs=[pl.BlockSpec((tm, d), lambda i, e: (i, 0)),
                  pl.BlockSpec((te, d), lambda i, e: (e, 0)),
                  pl.BlockSpec((te, d), lambda i, e: (e, 0)),
                  pl.BlockSpec((te // PEER_NKEYS, tm, PEER_NKEYS), lambda i, e: (e, i, 0)),
                  pl.BlockSpec((tm, d), lambda i, e: (i, 0)),
                  pl.BlockSpec((None, None, 1, d), lambda i, e: (row(i), gate_idx, 0, 0)),
                  pl.BlockSpec((1, d), lambda i, e: (0, 0))],
        out_specs=pl.BlockSpec((tm, d), lambda i, e: (i, 0)),
        out_shape=jax.ShapeDtypeStruct((n, d), F32),
        compiler_params=_cparams(("arbitrary", "arbitrary")),
        name="peer_dense",
    )(h, u16, v16, grows, x, mod, final_g.reshape(1, d))


def peer_block(x, norm_g, mod, w_q16, sk16, u16, v16, final_g, *, n_prompt, t_s, final_norm):
    n = x.shape[0]
    q, h = norm_matmul(x, norm_g, mod, 3, 4, w_q16, n_prompt=n_prompt, t_s=t_s, emit_h=True)
    idx, gates = peer_route(q, sk16)
    hk = PEER_HEADS * PEER_TOPK
    idx = idx.reshape(hk, n).T
    gates = gates.reshape(hk, n).T
    grows = peer_gate_rows(idx, gates)
    return peer_dense(h, u16, v16, grows, x, mod, 5, final_g, n_prompt=n_prompt, t_s=t_s,
                      final_norm=final_norm)


def kernel(x_prompt, x_sample, cache_a_k, cache_a_v, state_lru, cache_c_k, cache_c_v, c, c_ctx, w_mod, b_mod, norm1_g, norm2_g, final_norm_g, even_w_in, even_w_out, a_lambda, a_subln_g, lru_conv_w, lru_conv_b, lru_w_a, lru_b_a, lru_w_i, lru_b_i, lru_lambda, odd_w_in, odd_w_out, na_rpb, peer_w_q, peer_sub_keys, peer_u, peer_v):
    b_p, t_p, d = x_prompt.shape
    b_s, t_s, _ = x_sample.shape
    depth = w_mod.shape[0]
    n_p = b_p * t_p
    n_s = b_s * t_s
    lc = cache_a_k.shape[2]
    tok = dict(n_prompt=n_p, t_s=t_s)

    x = jnp.concatenate([x_prompt.reshape(n_p, d), x_sample.reshape(n_s, d)], axis=0)
    n_rows = -(-(1 + b_s) // SUBLANES) * SUBLANES
    cond = jnp.concatenate([c_ctx[None], c, jnp.zeros((n_rows - 1 - b_s, d), F32)], axis=0)
    mod_all = modulation_all(cond, w_mod, b_mod).reshape(depth, n_rows, 6, 1, d)
    rope = rope_tables(t_s)

    new_ak, new_av, new_lru, new_ck, new_cv = [], [], [], [], []
    for l in range(depth):
        mod = mod_all[l]
        if l % 2 == 0:
            e = l // 2
            lam_init = 0.8 - 0.6 * math.exp(-0.3 * l)
            proj = norm_matmul(x, norm1_g[l], mod, 0, 1, even_w_in[e].astype(BF16), **tok)
            ctx = (cache_a_k[:, e].reshape(b_s, lc, A_WIDTH), cache_a_v[:, e].reshape(b_s, lc, A_WIDTH))
            oa_p = diff_attention_call(proj, 0, b_p, t_p, a_lambda[e], a_subln_g[e], lam_init)
            oa_s = diff_attention_call(proj, n_p, b_s, t_s, a_lambda[e], a_subln_g[e], lam_init,
                                       ctx=ctx, rope=rope)
            a_f, b_f, a_b, b_b = lru_gates(proj, lru_conv_w[e], lru_conv_b[e], lru_w_a[e], lru_b_a[e],
                                           lru_w_i[e], lru_b_i[e], lru_lambda[e],
                                           n_prompt=n_p, t_p=t_p, t_s=t_s)
            zeros_p = jnp.zeros((b_p, LRU_WIDTH), F32)
            hf_p, fin_f = lru_scan(a_f, b_f, zeros_p, 0, b_p, t_p, reverse=False)
            ob_p, fin_b = lru_scan(a_b, b_b, zeros_p, 0, b_p, t_p, reverse=True, hf=hf_p, proj=proj)
            hf_s, _ = lru_scan(a_f, b_f, state_lru[:, e, 0], n_p, b_s, t_s, reverse=False)
            ob_s, _ = lru_scan(a_b, b_b, state_lru[:, e, 1], n_p, b_s, t_s, reverse=True,
                               hf=hf_s, proj=proj)
            o_a = jnp.concatenate([oa_p, oa_s], axis=0)
            o_b = jnp.concatenate([ob_p, ob_s], axis=0)
            w_out = even_w_out[e].astype(BF16)
            x = matmul_res([o_a, o_b], [w_out[:A_WIDTH], w_out[A_WIDTH:]], x, mod, 2, **tok)
            new_ak.append(proj[:n_p, A_WIDTH:2 * A_WIDTH].reshape(b_p, t_p, A_HEADS, 2 * A_QK_DIM))
            new_av.append(proj[:n_p, 2 * A_WIDTH:3 * A_WIDTH].reshape(b_p, t_p, A_HEADS, A_V_DIM))
            new_lru.append(jnp.stack([fin_f, fin_b], axis=1))
        else:
            o = l // 2
            proj = norm_matmul(x, norm1_g[l], mod, 0, 1, odd_w_in[o].astype(BF16), **tok)
            oc_p = attention_call(proj, b_p, t_p)
            bias = na_bias_tables(na_rpb[o], t_s // GRID_W)
            oc_s = na_call(proj, n_p, b_s, t_s, cache_c_k[:, o].reshape(b_s, lc, C_WIDTH),
                           cache_c_v[:, o].reshape(b_s, lc, C_WIDTH), bias)
            o_c = jnp.concatenate([oc_p, oc_s], axis=0)
            x = matmul_res([o_c], [odd_w_out[o].astype(BF16)], x, mod, 2, **tok)
            new_ck.append(proj[:n_p, C_WIDTH:2 * C_WIDTH].reshape(b_p, t_p, C_HEADS, C_HEAD_DIM))
            new_cv.append(proj[:n_p, 2 * C_WIDTH:3 * C_WIDTH].reshape(b_p, t_p, C_HEADS, C_HEAD_DIM))
        x = peer_block(x, norm2_g[l], mod, peer_w_q[l].astype(BF16), peer_sub_keys[l].astype(BF16),
                       peer_u[l].astype(BF16), peer_v[l].astype(BF16), final_norm_g,
                       final_norm=(l == depth - 1), **tok)
    y_prompt = x[:n_p].reshape(b_p, t_p, d)
    y_sample = x[n_p:].reshape(b_s, t_s, d)
    return (y_prompt, y_sample, jnp.stack(new_ak, axis=1), jnp.stack(new_av, axis=1),
            jnp.stack(new_lru, axis=1), jnp.stack(new_ck, axis=1), jnp.stack(new_cv, axis=1))
```

```python
import functools
import math

import numpy as np
import jax
import jax.numpy as jnp
from jax import lax
from jax.experimental import pallas as pl
from jax.experimental.pallas import tpu as pltpu

F32 = jnp.float32
BF16 = jnp.bfloat16

D_MODEL = 2048
GRID_W = 64
EPS = 1e-6
ROPE_BASE = 10000.0
A_HEADS = 8
A_QK_DIM = 64
A_V_DIM = 128
A_WIDTH = A_HEADS * A_V_DIM
LRU_WIDTH = 1024
LRU_BLOCKS = 8
LRU_BLOCK = LRU_WIDTH // LRU_BLOCKS
CONV_W = 4
LRU_C = 8.0
EVEN_IN = 3 * A_WIDTH + 2 * LRU_WIDTH
C_HEADS = 16
C_HEAD_DIM = 128
C_WIDTH = C_HEADS * C_HEAD_DIM
NA_KH = 8
NA_KW = 16
PEER_HEADS = 8
PEER_NKEYS = 128
PEER_N = PEER_NKEYS * PEER_NKEYS
PEER_QDIM = 256
PEER_TOPK = 16

LANES = 128
SUBLANES = 8
VMEM_LIMIT = 56 * 1024 * 1024

NT_DIMS = (((1,), (1,)), ((), ()))


def _cparams(sem):
    return pltpu.CompilerParams(dimension_semantics=sem, vmem_limit_bytes=VMEM_LIMIT)


def _gelu(x):
    c = math.sqrt(2.0 / math.pi)
    return 0.5 * x * (1.0 + jnp.tanh(c * (x + 0.044715 * (x * x * x))))


def _sigmoid(x):
    return 1.0 / (1.0 + jnp.exp(-x))


def _mod_kernel(c_ref, w_ref, b_ref, o_ref):
    c = c_ref[...]
    s = c * _sigmoid(c)
    o_ref[0] = jnp.dot(s.astype(BF16), w_ref[0].astype(BF16),
                       preferred_element_type=F32) + b_ref[0]


def modulation_all(cond, w_mod, b_mod):
    depth, d, n6 = w_mod.shape
    r = cond.shape[0]
    tn = 768
    return pl.pallas_call(
        _mod_kernel,
        grid=(depth, n6 // tn),
        in_specs=[pl.BlockSpec((r, d), lambda l, j: (0, 0)),
                  pl.BlockSpec((1, d, tn), lambda l, j: (l, 0, j)),
                  pl.BlockSpec((1, 1, tn), lambda l, j: (l, 0, j))],
        out_specs=pl.BlockSpec((1, r, tn), lambda l, j: (l, 0, j)),
        out_shape=jax.ShapeDtypeStruct((depth, r, n6), F32),
        compiler_params=_cparams(("arbitrary", "arbitrary")),
        name="modulation",
    )(cond, w_mod, b_mod.reshape(depth, 1, n6))


def _mod_row(i, tm, n_prompt, t_s):
    s = i * tm
    return jnp.where(s < n_prompt, 0, 1 + (s - n_prompt) // t_s)


def _norm_matmul_kernel(x_ref, g_ref, sh_ref, sc_ref, w_ref, o_ref, *rest, emit_h):
    if emit_h:
        h_out_ref, h_scr = rest
    else:
        (h_scr,) = rest

    @pl.when(pl.program_id(1) == 0)
    def _():
        x = x_ref[...]
        y = x * lax.rsqrt(jnp.mean(x * x, axis=-1, keepdims=True) + EPS)
        h = (y * g_ref[...]) * (1.0 + sc_ref[...]) + sh_ref[...]
        h_scr[...] = h.astype(BF16)
        if emit_h:
            h_out_ref[...] = h.astype(BF16)

    o_ref[...] = jnp.dot(h_scr[...], w_ref[...], preferred_element_type=F32).astype(o_ref.dtype)


def norm_matmul(x, g, mod, shift_idx, scale_idx, w, *, n_prompt, t_s, emit_h=False,
                tm=1024, tn=512, out_dtype=F32):
    n, d = x.shape
    nout = w.shape[1]
    row = functools.partial(_mod_row, tm=tm, n_prompt=n_prompt, t_s=t_s)
    out_shape = [jax.ShapeDtypeStruct((n, nout), out_dtype)]
    out_specs = [pl.BlockSpec((tm, tn), lambda i, j: (i, j))]
    if emit_h:
        out_shape.append(jax.ShapeDtypeStruct((n, d), BF16))
        out_specs.append(pl.BlockSpec((tm, d), lambda i, j: (i, 0)))
    res = pl.pallas_call(
        functools.partial(_norm_matmul_kernel, emit_h=emit_h),
        grid=(n // tm, nout // tn),
        in_specs=[pl.BlockSpec((tm, d), lambda i, j: (i, 0)),
                  pl.BlockSpec((1, d), lambda i, j: (0, 0)),
                  pl.BlockSpec((None, None, 1, d), lambda i, j: (row(i), shift_idx, 0, 0)),
                  pl.BlockSpec((None, None, 1, d), lambda i, j: (row(i), scale_idx, 0, 0)),
                  pl.BlockSpec((d, tn), lambda i, j: (0, j))],
        out_specs=out_specs,
        out_shape=out_shape,
        scratch_shapes=[pltpu.VMEM((tm, d), BF16)],
        compiler_params=_cparams(("arbitrary", "arbitrary")),
        name="norm_matmul",
    )(x, g.reshape(1, d), mod, mod, w)
    return res if emit_h else res[0]


def _matmul_res_kernel(*refs, n_in, np_tiles):
    ap_refs = refs[:n_in]
    as_refs = refs[n_in:2 * n_in]
    w_refs = refs[2 * n_in:3 * n_in]
    x_ref, gate_ref, o_ref = refs[3 * n_in:]

    def emit(a_refs):
        acc = None
        for a_ref, w_ref in zip(a_refs, w_refs):
            p = jnp.dot(a_ref[...].astype(BF16), w_ref[...], preferred_element_type=F32)
            acc = p if acc is None else acc + p
        o_ref[...] = x_ref[...] + gate_ref[...] * acc

    i = pl.program_id(0)
    pl.when(i < np_tiles)(lambda: emit(ap_refs))
    pl.when(i >= np_tiles)(lambda: emit(as_refs))


def matmul_res(a_pairs, w_list, x, mod, gate_idx, *, n_prompt, t_s, tm=1024, tn=512):
    n, d = x.shape
    n_in = len(a_pairs)
    np_tiles = n_prompt // tm
    ns_tiles = n // tm - np_tiles
    row = functools.partial(_mod_row, tm=tm, n_prompt=n_prompt, t_s=t_s)
    in_specs = ([pl.BlockSpec((tm, ap.shape[1]), lambda i, j: (jnp.minimum(i, np_tiles - 1), 0))
                 for ap, _ in a_pairs]
                + [pl.BlockSpec((tm, a_s.shape[1]),
                                lambda i, j: (jnp.clip(i - np_tiles, 0, ns_tiles - 1), 0))
                   for _, a_s in a_pairs]
                + [pl.BlockSpec((w.shape[0], tn), lambda i, j: (0, j)) for w in w_list]
                + [pl.BlockSpec((tm, tn), lambda i, j: (i, j)),
                   pl.BlockSpec((None, None, 1, tn), lambda i, j: (row(i), gate_idx, 0, j))])
    return pl.pallas_call(
        functools.partial(_matmul_res_kernel, n_in=n_in, np_tiles=np_tiles),
        grid=(n // tm, d // tn),
        in_specs=in_specs,
        out_specs=pl.BlockSpec((tm, tn), lambda i, j: (i, j)),
        out_shape=jax.ShapeDtypeStruct((n, d), F32),
        compiler_params=_cparams(("arbitrary", "arbitrary")),
        name="matmul_res",
    )(*[ap for ap, _ in a_pairs], *[a_s for _, a_s in a_pairs], *w_list, x, mod)


def _rope(x, cos, sin_signed):
    lane = lax.broadcasted_iota(jnp.int32, x.shape, 1)
    first = (lane % 32) < 16
    partner = jnp.where(first, pltpu.roll(x, LANES - 16, axis=1), pltpu.roll(x, 16, axis=1))
    return x * cos + partner * sin_signed


def _diff_attn_kernel(*refs, has_ctx, lam_init, tq, n_sub):
    if has_ctx:
        (q_ref, k_ref, v_ref, ck_ref, cv_ref, cos_ref, sin_ref, lam_ref, g_ref,
         o_ref, k_scr, v_scr, ck_scr, cv_scr) = refs
    else:
        q_ref, k_ref, v_ref, lam_ref, g_ref, o_ref, k_scr, v_scr = refs
    qi = pl.program_id(2)

    @pl.when(qi == 0)
    def _():
        k = k_ref[...]
        if has_ctx:
            k = _rope(k, cos_ref[...], sin_ref[...])
            ck_scr[...] = ck_ref[...].astype(BF16)
            cv_scr[...] = cv_ref[...].astype(BF16)
        k_scr[...] = k.astype(BF16)
        v_scr[...] = v_ref[...].astype(BF16)

    q = q_ref[...]
    if has_ctx:
        off = pl.multiple_of(qi * tq, tq)
        q = _rope(q, cos_ref[pl.ds(off, tq), :], sin_ref[pl.ds(off, tq), :])
    q = q * (A_QK_DIM ** -0.5)
    lp = lam_ref[...]
    lam = (jnp.exp(jnp.sum(lp[0:1] * lp[1:2], axis=-1, keepdims=True))
           - jnp.exp(jnp.sum(lp[2:3] * lp[3:4], axis=-1, keepdims=True)) + lam_init)
    ts = tq // n_sub
    lane = lax.broadcasted_iota(jnp.int32, (ts, LANES), 1)

    scored = []
    for a in range(n_sub):
        qa = q[a * ts:(a + 1) * ts]
        q2 = jnp.concatenate([jnp.where(lane < A_QK_DIM, qa, 0.0),
                              jnp.where(lane >= A_QK_DIM, qa, 0.0)], axis=0).astype(BF16)
        s = lax.dot_general(q2, k_scr[...], NT_DIMS, preferred_element_type=F32)
        mx = jnp.max(s, axis=-1, keepdims=True)
        sc = None
        if has_ctx:
            sc = lax.dot_general(q2, ck_scr[...], NT_DIMS, preferred_element_type=F32)
            mx = jnp.maximum(mx, jnp.max(sc, axis=-1, keepdims=True))
        scored.append((s, sc, mx))
    for a, (s, sc, mx) in enumerate(scored):
        e = jnp.exp(s - mx)
        l = jnp.sum(e, axis=-1, keepdims=True)
        if has_ctx:
            ec = jnp.exp(sc - mx)
            l = l + jnp.sum(ec, axis=-1, keepdims=True)
        r = 1.0 / l
        r0 = r[:ts]
        r1 = lam * r[ts:]
        d = jnp.dot((e[:ts] * r0 - e[ts:] * r1).astype(BF16), v_scr[...],
                    preferred_element_type=F32)
        if has_ctx:
            d = d + jnp.dot((ec[:ts] * r0 - ec[ts:] * r1).astype(BF16), cv_scr[...],
                            preferred_element_type=F32)
        y = d * lax.rsqrt(jnp.mean(d * d, axis=-1, keepdims=True) + EPS)
        o_ref[a * ts:(a + 1) * ts, :] = ((y * g_ref[...]) * (1.0 - lam_init)).astype(o_ref.dtype)


def diff_attention_call(proj, tok0, b, t, lam_p, subln_g, lam_init, ctx=None, rope=None, tq=256,
                        n_sub=2):
    tq = min(tq, t)
    nq = t // tq
    q0 = tok0 // tq
    k0 = tok0 // t
    has_ctx = ctx is not None
    in_specs = [pl.BlockSpec((tq, LANES), lambda bi, h, qi: (q0 + bi * nq + qi, h)),
                pl.BlockSpec((t, LANES), lambda bi, h, qi: (k0 + bi, A_HEADS + h)),
                pl.BlockSpec((t, LANES), lambda bi, h, qi: (k0 + bi, 2 * A_HEADS + h))]
    args = [proj, proj, proj]
    if has_ctx:
        ck, cv = ctx
        lc = ck.shape[1]
        cos, sin = rope
        in_specs += [pl.BlockSpec((None, lc, LANES), lambda bi, h, qi: (bi, 0, h)),
                     pl.BlockSpec((None, lc, LANES), lambda bi, h, qi: (bi, 0, h)),
                     pl.BlockSpec((t, LANES), lambda bi, h, qi: (0, 0)),
                     pl.BlockSpec((t, LANES), lambda bi, h, qi: (0, 0))]
        args += [ck, cv, cos, sin]
    in_specs += [pl.BlockSpec((4, A_QK_DIM), lambda bi, h, qi: (0, 0)),
                 pl.BlockSpec((1, A_V_DIM), lambda bi, h, qi: (0, 0))]
    args += [lam_p, subln_g.reshape(1, A_V_DIM)]
    scratch = [pltpu.VMEM((t, LANES), BF16), pltpu.VMEM((t, LANES), BF16)]
    if has_ctx:
        scratch += [pltpu.VMEM((lc, LANES), BF16), pltpu.VMEM((lc, LANES), BF16)]
    return pl.pallas_call(
        functools.partial(_diff_attn_kernel, has_ctx=has_ctx, lam_init=lam_init, tq=tq, n_sub=n_sub),
        grid=(b, A_HEADS, nq),
        in_specs=in_specs,
        out_specs=pl.BlockSpec((tq, LANES), lambda bi, h, qi: (bi * nq + qi, h)),
        out_shape=jax.ShapeDtypeStruct((b * t, A_WIDTH), BF16),
        scratch_shapes=scratch,
        compiler_params=_cparams(("arbitrary", "arbitrary", "arbitrary")),
        name="diff_attention",
    )(*args)


def rope_tables(t):
    tok = jnp.arange(t)
    row = (tok // GRID_W).astype(F32)
    col = (tok % GRID_W).astype(F32)
    n_freq = A_QK_DIM // 4
    inv = ROPE_BASE ** (-jnp.arange(n_freq, dtype=F32) / n_freq)
    ang_r = row[:, None] * inv
    ang_c = col[:, None] * inv
    cos64 = jnp.concatenate([jnp.cos(ang_r), jnp.cos(ang_r), jnp.cos(ang_c), jnp.cos(ang_c)], axis=1)
    sin64 = jnp.concatenate([-jnp.sin(ang_r), jnp.sin(ang_r), -jnp.sin(ang_c), jnp.sin(ang_c)], axis=1)
    return jnp.tile(cos64, (1, 2)), jnp.tile(sin64, (1, 2))


def _attn_kernel(q_ref, k_ref, v_ref, o_ref):
    q = q_ref[...].astype(BF16)
    k = k_ref[...].astype(BF16)
    s = lax.dot_general(q, k, NT_DIMS, preferred_element_type=F32) * (C_HEAD_DIM ** -0.5)
    mx = jnp.max(s, axis=-1, keepdims=True)
    e = jnp.exp(s - mx)
    l = jnp.sum(e, axis=-1, keepdims=True)
    o = jnp.dot(e.astype(BF16), v_ref[...].astype(BF16), preferred_element_type=F32)
    o_ref[...] = (o / l).astype(o_ref.dtype)


def attention_call(proj, b, t):
    return pl.pallas_call(
        _attn_kernel,
        grid=(b, C_HEADS),
        in_specs=[pl.BlockSpec((t, LANES), lambda bi, h: (bi, h)),
                  pl.BlockSpec((t, LANES), lambda bi, h: (bi, C_HEADS + h)),
                  pl.BlockSpec((t, LANES), lambda bi, h: (bi, 2 * C_HEADS + h))],
        out_specs=pl.BlockSpec((t, LANES), lambda bi, h: (bi, h)),
        out_shape=jax.ShapeDtypeStruct((b * t, C_WIDTH), BF16),
        compiler_params=_cparams(("arbitrary", "arbitrary")),
        name="ctx_attention",
    )(proj, proj, proj)


NA_QROWS = 8
NA_WROWS = 16
NA_CHUNK_ROWS = 4
NA_NCHUNK = NA_WROWS // NA_CHUNK_ROWS
NA_NSUB = 2


def _na_window_start(rb, rows):
    return np.clip(rb * NA_QROWS - NA_KH // 2, 0, rows - NA_WROWS)


def na_bias_tables(rpb, rows):
    n_rb = rows // NA_QROWS
    geos = []
    for rb in (0, 1, n_rb - 1):
        r0 = rb * NA_QROWS
        ws = _na_window_start(rb, rows)
        qr = r0 + np.arange(NA_QROWS)[:, None, None, None]
        qc = np.arange(GRID_W)[None, :, None, None]
        kr = ws + np.arange(NA_WROWS)[None, None, :, None]
        kc = np.arange(GRID_W)[None, None, None, :]
        rs = np.clip(qr - NA_KH // 2, 0, rows - NA_KH)
        cs = np.clip(qc - NA_KW // 2, 0, GRID_W - NA_KW)
        ok = (kr >= rs) & (kr < rs + NA_KH) & (kc >= cs) & (kc < cs + NA_KW)
        ri = np.clip(kr - qr + NA_KH - 1, 0, 2 * NA_KH - 2)
        ci = np.clip(kc - qc + NA_KW - 1, 0, 2 * NA_KW - 2)
        shape = (NA_QROWS, GRID_W, NA_WROWS, GRID_W)
        ok = np.broadcast_to(ok, shape).reshape(NA_QROWS * GRID_W, NA_WROWS * GRID_W)
        by_col = rpb[:, :, ci[0, :, 0, :]]
        tab = by_col[:, ri[:, 0, :, 0]]
        tab = tab.transpose(0, 1, 3, 2, 4).reshape((rpb.shape[0],) + ok.shape)
        geos.append(jnp.where(ok[None], tab, -jnp.inf))
    return jnp.stack(geos, axis=1).astype(F32)


def _na_kernel(*refs):
    q_ref = refs[0]
    k_refs = refs[1:1 + NA_NCHUNK]
    v_refs = refs[1 + NA_NCHUNK:1 + 2 * NA_NCHUNK]
    ck_ref, cv_ref, bias_ref, o_ref = refs[1 + 2 * NA_NCHUNK:]
    scale = C_HEAD_DIM ** -0.5
    k = jnp.concatenate([r[...].astype(BF16) for r in k_refs], axis=0)
    v = jnp.concatenate([r[...].astype(BF16) for r in v_refs], axis=0)
    ck = ck_ref[...].astype(BF16)
    cv = cv_ref[...].astype(BF16)
    ts = q_ref.shape[0] // NA_NSUB
    scored = []
    for a in range(NA_NSUB):
        rows = slice(a * ts, (a + 1) * ts)
        q = q_ref[rows, :].astype(BF16)
        s = lax.dot_general(q, k, NT_DIMS, preferred_element_type=F32) * scale + bias_ref[rows, :]
        sc = lax.dot_general(q, ck, NT_DIMS, preferred_element_type=F32) * scale
        mx = jnp.maximum(jnp.max(s, axis=-1, keepdims=True), jnp.max(sc, axis=-1, keepdims=True))
        scored.append((s, sc, mx))
    for a, (s, sc, mx) in enumerate(scored):
        e = jnp.exp(s - mx)
        ec = jnp.exp(sc - mx)
        l = jnp.sum(e, axis=-1, keepdims=True) + jnp.sum(ec, axis=-1, keepdims=True)
        o = (jnp.dot(e.astype(BF16), v, preferred_element_type=F32)
             + jnp.dot(ec.astype(BF16), cv, preferred_element_type=F32))
        o_ref[a * ts:(a + 1) * ts, :] = (o / l).astype(o_ref.dtype)


def na_call(proj, tok0, b, t, ck, cv, bias):
    rows = t // GRID_W
    n_rb = rows // NA_QROWS
    qtok = NA_QROWS * GRID_W
    ctok = NA_CHUNK_ROWS * GRID_W
    lc = ck.shape[1]
    q0 = tok0 // qtok
    c0 = tok0 // ctok
    cps = t // ctok

    def wchunk(rb):
        return jnp.clip(rb * (NA_QROWS // NA_CHUNK_ROWS) - 1, 0, (rows - NA_WROWS) // NA_CHUNK_ROWS)

    def geo(rb):
        return jnp.where(rb == 0, 0, jnp.where(rb == n_rb - 1, 2, 1))

    def kspec(c, colblk):
        return pl.BlockSpec((ctok, LANES),
                            lambda h, rb, bi: (c0 + bi * cps + wchunk(rb) + c, colblk + h))

    in_specs = ([pl.BlockSpec((qtok, LANES), lambda h, rb, bi: (q0 + bi * n_rb + rb, h))]
                + [kspec(c, C_HEADS) for c in range(NA_NCHUNK)]
                + [kspec(c, 2 * C_HEADS) for c in range(NA_NCHUNK)]
                + [pl.BlockSpec((None, lc, LANES), lambda h, rb, bi: (bi, 0, h)),
                   pl.BlockSpec((None, lc, LANES), lambda h, rb, bi: (bi, 0, h)),
                   pl.BlockSpec((None, None, qtok, NA_WROWS * GRID_W),
                                lambda h, rb, bi: (h, geo(rb), 0, 0))])
    return pl.pallas_call(
        _na_kernel,
        grid=(C_HEADS, n_rb, b),
        in_specs=in_specs,
        out_specs=pl.BlockSpec((qtok, LANES), lambda h, rb, bi: (bi * n_rb + rb, h)),
        out_shape=jax.ShapeDtypeStruct((b * t, C_WIDTH), BF16),
        compiler_params=_cparams(("arbitrary", "arbitrary", "arbitrary")),
        name="neighbourhood_attention",
    )(*([proj] * (1 + 2 * NA_NCHUNK)), ck, cv, bias)


LRU_HALO = SUBLANES


def _expm1(x):
    u = jnp.exp(x)
    return jnp.where(u == 1.0, x, (u - 1.0) * x / jnp.log(u))


def _lru_gates_kernel(x_ref, prev_ref, next_ref, cw_ref, cb_ref, wa_ref, ba_ref, wi_ref, bi_ref,
                      lam_ref, af_ref, bf_ref, ab_ref, bb_ref, *, tt, n_prompt, t_p, t_s):
    i = pl.program_id(0)
    s = i * tt
    in_prompt = s < n_prompt
    pos = jnp.where(in_prompt, s % t_p, (s - n_prompt) % t_s)
    seq_len = jnp.where(in_prompt, t_p, t_s)
    is_start = pos == 0
    is_end = pos + tt == seq_len
    x = x_ref[...]
    prev = jnp.where(is_start, 0.0, prev_ref[LRU_HALO - 2:LRU_HALO, :])
    nxt = jnp.where(is_end, 0.0, next_ref[0:1, :])
    xp = jnp.concatenate([prev, x, nxt], axis=0)
    cw = cw_ref[...]
    xc = cb_ref[...] + xp[0:tt] * cw[0:1]
    for j in range(1, CONV_W):
        xc = xc + xp[j:j + tt] * cw[j:j + 1]
    outs = ((af_ref, bf_ref), (ab_ref, bb_ref))
    for n in range(LRU_BLOCKS):
        sl = slice(n * LRU_BLOCK, (n + 1) * LRU_BLOCK)
        xb = xc[:, sl]
        xb16 = xb.astype(BF16)
        for dr in range(2):
            r = _sigmoid(jnp.dot(xb16, wa_ref[dr, n], preferred_element_type=F32) + ba_ref[dr:dr + 1, sl])
            g = _sigmoid(jnp.dot(xb16, wi_ref[dr, n], preferred_element_type=F32) + bi_ref[dr:dr + 1, sl])
            z = -lam_ref[dr:dr + 1, sl]
            softplus = jnp.maximum(z, 0.0) + jnp.log(1.0 + jnp.exp(-jnp.abs(z)))
            log_a = -LRU_C * r * softplus
            a_out, b_out = outs[dr]
            a_out[:, sl] = jnp.exp(log_a)
            b_out[:, sl] = jnp.sqrt(-_expm1(2.0 * log_a)) * g * xb


def lru_gates(proj, conv_w, conv_b, w_a, b_a, w_i, b_i, lam, *, n_prompt, t_p, t_s, tt=256):
    n = proj.shape[0]
    xcol = (3 * A_WIDTH) // LRU_WIDTH
    hb = tt // LRU_HALO
    nh = n // LRU_HALO
    full = lambda shape: pl.BlockSpec(shape, lambda i: (0,) * len(shape))
    out = jax.ShapeDtypeStruct((n, LRU_WIDTH), F32)
    ospec = pl.BlockSpec((tt, LRU_WIDTH), lambda i: (i, 0))
    return pl.pallas_call(
        functools.partial(_lru_gates_kernel, tt=tt, n_prompt=n_prompt, t_p=t_p, t_s=t_s),
        grid=(n // tt,),
        in_specs=[pl.BlockSpec((tt, LRU_WIDTH), lambda i: (i, xcol)),
                  pl.BlockSpec((LRU_HALO, LRU_WIDTH), lambda i: (jnp.maximum(i * hb - 1, 0), xcol)),
                  pl.BlockSpec((LRU_HALO, LRU_WIDTH), lambda i: (jnp.minimum((i + 1) * hb, nh - 1), xcol)),
                  full((CONV_W, LRU_WIDTH)), full((1, LRU_WIDTH)),
                  full((2, LRU_BLOCKS, LRU_BLOCK, LRU_BLOCK)), full((2, LRU_WIDTH)),
                  full((2, LRU_BLOCKS, LRU_BLOCK, LRU_BLOCK)), full((2, LRU_WIDTH)),
                  full((2, LRU_WIDTH))],
        out_specs=[ospec] * 4,
        out_shape=[out] * 4,
        compiler_params=_cparams(("arbitrary",)),
        name="lru_gates",
    )(proj, proj, proj, conv_w, conv_b.reshape(1, LRU_WIDTH), w_a.astype(BF16), b_a,
      w_i.astype(BF16), b_i, lam)


def _lru_scan_kernel(*refs, tt, reverse, combine):
    if combine:
        a_ref, b_ref, h0_ref, hf_ref, g_ref, o_ref, fin_ref, h_scr = refs
    else:
        a_ref, b_ref, h0_ref, o_ref, fin_ref, h_scr = refs
    ti = pl.program_id(1)

    @pl.when(ti == 0)
    def _():
        h_scr[...] = h0_ref[...]

    def step(j, h):
        t = tt - 1 - j if reverse else j
        h = a_ref[t] * h + b_ref[t]
        if combine:
            o_ref[t] = ((hf_ref[t] + h) * _gelu(g_ref[t])).astype(o_ref.dtype)
        else:
            o_ref[t] = h
        return h

    h = lax.fori_loop(0, tt, step, h_scr[...], unroll=8)
    h_scr[...] = h
    fin_ref[...] = h


def lru_scan(a, b, h0, tok0, nb, t, *, reverse, hf=None, proj=None, tt=256):
    tt = min(tt, t)
    nt = t // tt
    r0 = tok0 // tt
    combine = hf is not None
    vreg = (SUBLANES, LANES)
    view = lambda z: z.reshape(z.shape[0], LRU_WIDTH // LANES, LANES)

    def tmap(bi, ti):
        return nt - 1 - ti if reverse else ti

    in_specs = [pl.BlockSpec((tt,) + vreg, lambda bi, ti: (r0 + bi * nt + tmap(bi, ti), 0, 0)),
                pl.BlockSpec((tt,) + vreg, lambda bi, ti: (r0 + bi * nt + tmap(bi, ti), 0, 0)),
                pl.BlockSpec((None,) + vreg, lambda bi, ti: (bi, 0, 0))]
    args = [view(a), view(b), view(h0)]
    if combine:
        gcol = (3 * A_WIDTH + LRU_WIDTH) // LRU_WIDTH
        in_specs += [pl.BlockSpec((tt,) + vreg, lambda bi, ti: (bi * nt + tmap(bi, ti), 0, 0)),
                     pl.BlockSpec((tt,) + vreg, lambda bi, ti: (r0 + bi * nt + tmap(bi, ti), gcol, 0))]
        args += [view(hf), proj.reshape(proj.shape[0], EVEN_IN // LANES, LANES)]
    out, fin = pl.pallas_call(
        functools.partial(_lru_scan_kernel, tt=tt, reverse=reverse, combine=combine),
        grid=(nb, nt),
        in_specs=in_specs,
        out_specs=[pl.BlockSpec((tt,) + vreg, lambda bi, ti: (bi * nt + tmap(bi, ti), 0, 0)),
                   pl.BlockSpec((None,) + vreg, lambda bi, ti: (bi, 0, 0))],
        out_shape=[jax.ShapeDtypeStruct((nb * t,) + vreg, F32),
                   jax.ShapeDtypeStruct((nb,) + vreg, F32)],
        scratch_shapes=[pltpu.VMEM(vreg, F32)],
        compiler_params=_cparams(("arbitrary", "arbitrary")),
        name="lru_scan_bwd" if reverse else "lru_scan_fwd",
    )(*args)
    return out.reshape(nb * t, LRU_WIDTH), fin.reshape(nb, LRU_WIDTH)


def _topk_rows(s_list, k, payloads=None):
    n = s_list[0].shape[0]
    iota = lax.broadcasted_iota(jnp.int32, s_list[0].shape, 0).astype(F32)
    s_list = list(s_list)
    vals = [[] for _ in s_list]
    idxs = [[] for _ in s_list]
    for _ in range(k):
        for j, s in enumerate(s_list):
            m = jnp.max(s, axis=0, keepdims=True)
            am = jnp.min(jnp.where(s == m, iota, float(n)), axis=0, keepdims=True)
            sel = iota == am
            vals[j].append(m)
            if payloads is None:
                idxs[j].append(am)
            else:
                idxs[j].append(jnp.sum(jnp.where(sel, payloads[j], 0.0), axis=0, keepdims=True))
            s_list[j] = jnp.where(sel, -jnp.inf, s)
    return [(jnp.concatenate(v, axis=0), jnp.concatenate(i, axis=0)) for v, i in zip(vals, idxs)]


def _peer_route_kernel(q_ref, sk_ref, idx_ref, gate_ref, *, tm):
    half = PEER_QDIM // 2
    n_chunk = tm // LANES
    scores = []
    for c in range(n_chunk):
        q = q_ref[c * LANES:(c + 1) * LANES, :].astype(BF16)
        for p in range(2):
            scores.append(lax.dot_general(sk_ref[p], q[:, p * half:(p + 1) * half], NT_DIMS,
                                          preferred_element_type=F32))
    tops = _topk_rows(scores, PEER_TOPK)
    nb = [PEER_TOPK // (a + 1) for a in range(PEER_TOPK)]
    pad = -sum(nb) % SUBLANES
    cands, cidxs = [], []
    for c in range(n_chunk):
        (s1, i1), (s2, i2) = tops[2 * c], tops[2 * c + 1]
        cands.append(jnp.concatenate([s1[a:a + 1] + s2[:nb[a]] for a in range(PEER_TOPK)]
                                     + [jnp.full((pad, LANES), -jnp.inf, F32)], axis=0))
        cidxs.append(jnp.concatenate(
            [i1[a:a + 1] * float(PEER_NKEYS) + i2[:nb[a]] for a in range(PEER_TOPK)]
            + [jnp.zeros((pad, LANES), F32)], axis=0))
    best = _topk_rows(cands, PEER_TOPK, payloads=cidxs)
    for c, (bs, bidx) in enumerate(best):
        e = jnp.exp(bs - bs[0:1])
        gates = e / jnp.sum(e, axis=0, keepdims=True)
        idx_ref[:, c * LANES:(c + 1) * LANES] = bidx.astype(jnp.int32)
        gate_ref[:, c * LANES:(c + 1) * LANES] = gates


def peer_route(q, sub_keys, tm=512):
    n = q.shape[0]
    half = PEER_QDIM // 2
    return pl.pallas_call(
        functools.partial(_peer_route_kernel, tm=tm),
        grid=(n // tm, PEER_HEADS),
        in_specs=[pl.BlockSpec((tm, PEER_QDIM), lambda i, h: (i, h)),
                  pl.BlockSpec((None, 2, PEER_NKEYS, half), lambda i, h: (h, 0, 0, 0))],
        out_specs=[pl.BlockSpec((None, PEER_TOPK, tm), lambda i, h: (h, 0, i))] * 2,
        out_shape=[jax.ShapeDtypeStruct((PEER_HEADS, PEER_TOPK, n), jnp.int32),
                   jax.ShapeDtypeStruct((PEER_HEADS, PEER_TOPK, n), F32)],
        compiler_params=_cparams(("arbitrary", "arbitrary")),
        name="peer_route",
    )(q, sub_keys)


def _peer_gates_kernel(idx_ref, gate_ref, g_ref):
    idx = idx_ref[...]
    gate = gate_ref[...]
    tg, _, hk = idx.shape
    i1 = (idx // PEER_NKEYS).astype(F32).astype(BF16)
    i2 = (idx % PEER_NKEYS).astype(F32).astype(BF16)
    iota = lax.broadcasted_iota(jnp.int32, (1, PEER_NKEYS, hk), 1).astype(F32).astype(BF16)
    zero = jnp.zeros((), BF16)
    a = jnp.where(i1 == iota, gate.astype(BF16), zero)
    bsel = jnp.where(i2 == iota, jnp.ones((), BF16), zero)
    g3 = lax.dot_general(a, bsel, (((2,), (2,)), ((0,), (0,))), preferred_element_type=F32)
    g_ref[...] = pltpu.einshape("nij->inj", g3).astype(g_ref.dtype)


def peer_gate_rows(idx, gates, tg=64):
    n, hk = idx.shape
    return pl.pallas_call(
        _peer_gates_kernel,
        grid=(n // tg,),
        in_specs=[pl.BlockSpec((tg, 1, hk), lambda i: (i, 0, 0))] * 2,
        out_specs=pl.BlockSpec((PEER_NKEYS, tg, PEER_NKEYS), lambda i: (0, i, 0)),
        out_shape=jax.ShapeDtypeStruct((PEER_NKEYS, n, PEER_NKEYS), BF16),
        compiler_params=_cparams(("arbitrary",)),
        name="peer_gate_rows",
    )(idx.reshape(n, 1, hk), gates.reshape(n, 1, hk))


def _peer_dense_kernel(h_ref, u_ref, v_ref, gr_ref, x_ref, mg_ref, fg_ref, o_ref, *, final_norm,
                       row_split):
    e = pl.program_id(1)

    @pl.when(e == 0)
    def _():
        o_ref[...] = jnp.zeros_like(o_ref)

    tr = h_ref.shape[0] // row_split
    scores = [lax.dot_general(h_ref[r * tr:(r + 1) * tr, :], u_ref[...], NT_DIMS,
                              preferred_element_type=F32) for r in range(row_split)]
    for r, s in enumerate(scores):
        rows = slice(r * tr, (r + 1) * tr)
        act = jnp.concatenate(
            [(_gelu(s[:, j * PEER_NKEYS:(j + 1) * PEER_NKEYS])
              * gr_ref[j, rows, :].astype(F32)).astype(BF16)
             for j in range(gr_ref.shape[0])], axis=1)
        o_ref[rows, :] += jnp.dot(act, v_ref[...], preferred_element_type=F32)

    @pl.when(e == pl.num_programs(1) - 1)
    def _():
        y = x_ref[...] + mg_ref[...] * o_ref[...]
        if final_norm:
            y = y * lax.rsqrt(jnp.mean(y * y, axis=-1, keepdims=True) + EPS) * fg_ref[...]
        o_ref[...] = y


def peer_dense(h, u16, v16, grows, x, mod, gate_idx, final_g, *, n_prompt, t_s, final_norm,
               tm=512, te=1024, row_split=2):
    n, d = x.shape
    ne = u16.shape[0]
    row = functools.partial(_mod_row, tm=tm, n_prompt=n_prompt, t_s=t_s)
    return pl.pallas_call(
        functools.partial(_peer_dense_kernel, final_norm=final_norm, row_split=row_split),
        grid=(n // tm, ne // te),
        in_specs=[pl.BlockSpec((tm, d), lambda i, e: (i, 0)),
                  pl.BlockSpec((te, d), lambda i, e: (e, 0)),
                  pl.BlockSpec((te, d), lambda i, e: (e, 0)),
                  pl.BlockSpec((te // PEER_NKEYS, tm, PEER_NKEYS), lambda i, e: (e, i, 0)),
                  pl.BlockSpec((tm, d), lambda i, e: (i, 0)),
                  pl.BlockSpec((None, None, 1, d), lambda i, e: (row(i), gate_idx, 0, 0)),
                  pl.BlockSpec((1, d), lambda i, e: (0, 0))],
        out_specs=pl.BlockSpec((tm, d), lambda i, e: (i, 0)),
        out_shape=jax.ShapeDtypeStruct((n, d), F32),
        compiler_params=_cparams(("arbitrary", "arbitrary")),
        name="peer_dense",
    )(h, u16, v16, grows, x, mod, final_g.reshape(1, d))


def peer_block(x, norm_g, mod, w_q16, sk16, u16, v16, final_g, *, n_prompt, t_s, final_norm):
    n = x.shape[0]
    q, h = norm_matmul(x, norm_g, mod, 3, 4, w_q16, n_prompt=n_prompt, t_s=t_s, emit_h=True)
    idx, gates = peer_route(q, sk16)
    hk = PEER_HEADS * PEER_TOPK
    idx = idx.reshape(hk, n).T
    gates = gates.reshape(hk, n).T
    grows = peer_gate_rows(idx, gates)
    return peer_dense(h, u16, v16, grows, x, mod, 5, final_g, n_prompt=n_prompt, t_s=t_s,
                      final_norm=final_norm)


def kernel(x_prompt, x_sample, cache_a_k, cache_a_v, state_lru, cache_c_k, cache_c_v, c, c_ctx, w_mod, b_mod, norm1_g, norm2_g, final_norm_g, even_w_in, even_w_out, a_lambda, a_subln_g, lru_conv_w, lru_conv_b, lru_w_a, lru_b_a, lru_w_i, lru_b_i, lru_lambda, odd_w_in, odd_w_out, na_rpb, peer_w_q, peer_sub_keys, peer_u, peer_v):
    b_p, t_p, d = x_prompt.shape
    b_s, t_s, _ = x_sample.shape
    depth = w_mod.shape[0]
    n_p = b_p * t_p
    n_s = b_s * t_s
    lc = cache_a_k.shape[2]
    tok = dict(n_prompt=n_p, t_s=t_s)

    x = jnp.concatenate([x_prompt.reshape(n_p, d), x_sample.reshape(n_s, d)], axis=0)
    n_rows = -(-(1 + b_s) // SUBLANES) * SUBLANES
    cond = jnp.concatenate([c_ctx[None], c, jnp.zeros((n_rows - 1 - b_s, d), F32)], axis=0)
    mod_all = modulation_all(cond, w_mod, b_mod).reshape(depth, n_rows, 6, 1, d)
    rope = rope_tables(t_s)

    new_ak, new_av, new_lru, new_ck, new_cv = [], [], [], [], []
    for l in range(depth):
        mod = mod_all[l]
        if l % 2 == 0:
            e = l // 2
            lam_init = 0.8 - 0.6 * math.exp(-0.3 * l)
            proj = norm_matmul(x, norm1_g[l], mod, 0, 1, even_w_in[e].astype(BF16), **tok)
            ctx = (cache_a_k[:, e].reshape(b_s, lc, A_WIDTH), cache_a_v[:, e].reshape(b_s, lc, A_WIDTH))
            oa_p = diff_attention_call(proj, 0, b_p, t_p, a_lambda[e], a_subln_g[e], lam_init)
            oa_s = diff_attention_call(proj, n_p, b_s, t_s, a_lambda[e], a_subln_g[e], lam_init,
                                       ctx=ctx, rope=rope)
            a_f, b_f, a_b, b_b = lru_gates(proj, lru_conv_w[e], lru_conv_b[e], lru_w_a[e], lru_b_a[e],
                                           lru_w_i[e], lru_b_i[e], lru_lambda[e],
                                           n_prompt=n_p, t_p=t_p, t_s=t_s)
            zeros_p = jnp.zeros((b_p, LRU_WIDTH), F32)
            hf_p, fin_f = lru_scan(a_f, b_f, zeros_p, 0, b_p, t_p, reverse=False)
            ob_p, fin_b = lru_scan(a_b, b_b, zeros_p, 0, b_p, t_p, reverse=True, hf=hf_p, proj=proj)
            hf_s, _ = lru_scan(a_f, b_f, state_lru[:, e, 0], n_p, b_s, t_s, reverse=False)
            ob_s, _ = lru_scan(a_b, b_b, state_lru[:, e, 1], n_p, b_s, t_s, reverse=True,
                               hf=hf_s, proj=proj)
            w_out = even_w_out[e].astype(BF16)
            x = matmul_res([(oa_p, oa_s), (ob_p, ob_s)], [w_out[:A_WIDTH], w_out[A_WIDTH:]],
                           x, mod, 2, **tok)
            new_ak.append(proj[:n_p, A_WIDTH:2 * A_WIDTH].reshape(b_p, t_p, A_HEADS, 2 * A_QK_DIM))
            new_av.append(proj[:n_p, 2 * A_WIDTH:3 * A_WIDTH].reshape(b_p, t_p, A_HEADS, A_V_DIM))
            new_lru.append(jnp.stack([fin_f, fin_b], axis=1))
        else:
            o = l // 2
            proj = norm_matmul(x, norm1_g[l], mod, 0, 1, odd_w_in[o].astype(BF16), **tok)
            oc_p = attention_call(proj, b_p, t_p)
            bias = na_bias_tables(na_rpb[o], t_s // GRID_W)
            oc_s = na_call(proj, n_p, b_s, t_s, cache_c_k[:, o].reshape(b_s, lc, C_WIDTH),
                           cache_c_v[:, o].reshape(b_s, lc, C_WIDTH), bias)
            x = matmul_res([(oc_p, oc_s)], [odd_w_out[o].astype(BF16)], x, mod, 2, **tok)
            new_ck.append(proj[:n_p, C_WIDTH:2 * C_WIDTH].reshape(b_p, t_p, C_HEADS, C_HEAD_DIM))
            new_cv.append(proj[:n_p, 2 * C_WIDTH:3 * C_WIDTH].reshape(b_p, t_p, C_HEADS, C_HEAD_DIM))
        x = peer_block(x, norm2_g[l], mod, peer_w_q[l].astype(BF16), peer_sub_keys[l].astype(BF16),
                       peer_u[l].astype(BF16), peer_v[l].astype(BF16), final_norm_g,
                       final_norm=(l == depth - 1), **tok)
    y_prompt = x[:n_p].reshape(b_p, t_p, d)
    y_sample = x[n_p:].reshape(b_s, t_s, d)
    return (y_prompt, y_sample, jnp.stack(new_ak, axis=1), jnp.stack(new_av, axis=1),
            jnp.stack(new_lru, axis=1), jnp.stack(new_ck, axis=1), jnp.stack(new_cv, axis=1))
```

```python
import functools
import math

import numpy as np
import jax
import jax.numpy as jnp
from jax import lax
from jax.experimental import pallas as pl
from jax.experimental.pallas import tpu as pltpu

F32 = jnp.float32
BF16 = jnp.bfloat16

D_MODEL = 2048
GRID_W = 64
EPS = 1e-6
ROPE_BASE = 10000.0
A_HEADS = 8
A_QK_DIM = 64
A_V_DIM = 128
A_WIDTH = A_HEADS * A_V_DIM
LRU_WIDTH = 1024
LRU_BLOCKS = 8
LRU_BLOCK = LRU_WIDTH // LRU_BLOCKS
CONV_W = 4
LRU_C = 8.0
EVEN_IN = 3 * A_WIDTH + 2 * LRU_WIDTH
C_HEADS = 16
C_HEAD_DIM = 128
C_WIDTH = C_HEADS * C_HEAD_DIM
NA_KH = 8
NA_KW = 16
PEER_HEADS = 8
PEER_NKEYS = 128
PEER_N = PEER_NKEYS * PEER_NKEYS
PEER_QDIM = 256
PEER_TOPK = 16

LANES = 128
SUBLANES = 8
VMEM_LIMIT = 56 * 1024 * 1024

NT_DIMS = (((1,), (1,)), ((), ()))


def _cparams(sem):
    return pltpu.CompilerParams(dimension_semantics=sem, vmem_limit_bytes=VMEM_LIMIT)


def _gelu(x):
    c = math.sqrt(2.0 / math.pi)
    return 0.5 * x * (1.0 + jnp.tanh(c * (x + 0.044715 * (x * x * x))))


def _sigmoid(x):
    return 1.0 / (1.0 + jnp.exp(-x))


def _mod_kernel(c_ref, w_ref, b_ref, o_ref):
    c = c_ref[...]
    s = c * _sigmoid(c)
    o_ref[0] = jnp.dot(s.astype(BF16), w_ref[0].astype(BF16),
                       preferred_element_type=F32) + b_ref[0]


def modulation_all(cond, w_mod, b_mod):
    depth, d, n6 = w_mod.shape
    r = cond.shape[0]
    tn = 768
    return pl.pallas_call(
        _mod_kernel,
        grid=(depth, n6 // tn),
        in_specs=[pl.BlockSpec((r, d), lambda l, j: (0, 0)),
                  pl.BlockSpec((1, d, tn), lambda l, j: (l, 0, j)),
                  pl.BlockSpec((1, 1, tn), lambda l, j: (l, 0, j))],
        out_specs=pl.BlockSpec((1, r, tn), lambda l, j: (l, 0, j)),
        out_shape=jax.ShapeDtypeStruct((depth, r, n6), F32),
        compiler_params=_cparams(("arbitrary", "arbitrary")),
        name="modulation",
    )(cond, w_mod, b_mod.reshape(depth, 1, n6))


def _mod_row(i, tm, n_prompt, t_s):
    s = i * tm
    return jnp.where(s < n_prompt, 0, 1 + (s - n_prompt) // t_s)


def _norm_matmul_kernel(x_ref, g_ref, sh_ref, sc_ref, w_ref, o_ref, *rest, emit_h):
    if emit_h:
        h_out_ref, h_scr = rest
    else:
        (h_scr,) = rest

    @pl.when(pl.program_id(1) == 0)
    def _():
        x = x_ref[...]
        y = x * lax.rsqrt(jnp.mean(x * x, axis=-1, keepdims=True) + EPS)
        h = (y * g_ref[...]) * (1.0 + sc_ref[...]) + sh_ref[...]
        h_scr[...] = h.astype(BF16)
        if emit_h:
            h_out_ref[...] = h.astype(BF16)

    o_ref[...] = jnp.dot(h_scr[...], w_ref[...], preferred_element_type=F32).astype(o_ref.dtype)


def norm_matmul(x, g, mod, shift_idx, scale_idx, w, *, n_prompt, t_s, emit_h=False,
                tm=1024, tn=1024, out_dtype=F32):
    n, d = x.shape
    nout = w.shape[1]
    row = functools.partial(_mod_row, tm=tm, n_prompt=n_prompt, t_s=t_s)
    out_shape = [jax.ShapeDtypeStruct((n, nout), out_dtype)]
    out_specs = [pl.BlockSpec((tm, tn), lambda i, j: (i, j))]
    if emit_h:
        out_shape.append(jax.ShapeDtypeStruct((n, d), BF16))
        out_specs.append(pl.BlockSpec((tm, d), lambda i, j: (i, 0)))
    res = pl.pallas_call(
        functools.partial(_norm_matmul_kernel, emit_h=emit_h),
        grid=(n // tm, nout // tn),
        in_specs=[pl.BlockSpec((tm, d), lambda i, j: (i, 0)),
                  pl.BlockSpec((1, d), lambda i, j: (0, 0)),
                  pl.BlockSpec((None, None, 1, d), lambda i, j: (row(i), shift_idx, 0, 0)),
                  pl.BlockSpec((None, None, 1, d), lambda i, j: (row(i), scale_idx, 0, 0)),
                  pl.BlockSpec((d, tn), lambda i, j: (0, j))],
        out_specs=out_specs,
        out_shape=out_shape,
        scratch_shapes=[pltpu.VMEM((tm, d), BF16)],
        compiler_params=_cparams(("arbitrary", "arbitrary")),
        name="norm_matmul",
    )(x, g.reshape(1, d), mod, mod, w)
    return res if emit_h else res[0]


def _matmul_res_kernel(*refs, n_in, np_tiles):
    ap_refs = refs[:n_in]
    as_refs = refs[n_in:2 * n_in]
    w_refs = refs[2 * n_in:3 * n_in]
    x_ref, gate_ref, o_ref = refs[3 * n_in:]

    def emit(a_refs):
        acc = None
        for a_ref, w_ref in zip(a_refs, w_refs):
            p = jnp.dot(a_ref[...].astype(BF16), w_ref[...], preferred_element_type=F32)
            acc = p if acc is None else acc + p
        o_ref[...] = x_ref[...] + gate_ref[...] * acc

    i = pl.program_id(0)
    pl.when(i < np_tiles)(lambda: emit(ap_refs))
    pl.when(i >= np_tiles)(lambda: emit(as_refs))


def matmul_res(a_pairs, w_list, x, mod, gate_idx, *, n_prompt, t_s, tm=1024, tn=512):
    n, d = x.shape
    n_in = len(a_pairs)
    np_tiles = n_prompt // tm
    ns_tiles = n // tm - np_tiles
    row = functools.partial(_mod_row, tm=tm, n_prompt=n_prompt, t_s=t_s)
    in_specs = ([pl.BlockSpec((tm, ap.shape[1]), lambda i, j: (jnp.minimum(i, np_tiles - 1), 0))
                 for ap, _ in a_pairs]
                + [pl.BlockSpec((tm, a_s.shape[1]),
                                lambda i, j: (jnp.clip(i - np_tiles, 0, ns_tiles - 1), 0))
                   for _, a_s in a_pairs]
                + [pl.BlockSpec((w.shape[0], tn), lambda i, j: (0, j)) for w in w_list]
                + [pl.BlockSpec((tm, tn), lambda i, j: (i, j)),
                   pl.BlockSpec((None, None, 1, tn), lambda i, j: (row(i), gate_idx, 0, j))])
    return pl.pallas_call(
        functools.partial(_matmul_res_kernel, n_in=n_in, np_tiles=np_tiles),
        grid=(n // tm, d // tn),
        in_specs=in_specs,
        out_specs=pl.BlockSpec((tm, tn), lambda i, j: (i, j)),
        out_shape=jax.ShapeDtypeStruct((n, d), F32),
        compiler_params=_cparams(("arbitrary", "arbitrary")),
        name="matmul_res",
    )(*[ap for ap, _ in a_pairs], *[a_s for _, a_s in a_pairs], *w_list, x, mod)


def _rope(x, cos, sin_signed):
    lane = lax.broadcasted_iota(jnp.int32, x.shape, 1)
    first = (lane % 32) < 16
    partner = jnp.where(first, pltpu.roll(x, LANES - 16, axis=1), pltpu.roll(x, 16, axis=1))
    return x * cos + partner * sin_signed


def _diff_attn_kernel(*refs, has_ctx, lam_init, tq, n_sub):
    if has_ctx:
        (q_ref, k_ref, v_ref, ck_ref, cv_ref, cos_ref, sin_ref, lam_ref, g_ref,
         o_ref, k_scr, v_scr, ck_scr, cv_scr) = refs
    else:
        q_ref, k_ref, v_ref, lam_ref, g_ref, o_ref, k_scr, v_scr = refs
    qi = pl.program_id(2)

    @pl.when(qi == 0)
    def _():
        k = k_ref[...]
        if has_ctx:
            k = _rope(k, cos_ref[...], sin_ref[...])
            ck_scr[...] = ck_ref[...].T.astype(BF16)
            cv_scr[...] = cv_ref[...].astype(BF16)
        k_scr[...] = k.T.astype(BF16)
        v_scr[...] = v_ref[...].astype(BF16)

    q = q_ref[...]
    if has_ctx:
        off = pl.multiple_of(qi * tq, tq)
        q = _rope(q, cos_ref[pl.ds(off, tq), :], sin_ref[pl.ds(off, tq), :])
    q = q * (A_QK_DIM ** -0.5)
    lp = lam_ref[...]
    lam = (jnp.exp(jnp.sum(lp[0:1] * lp[1:2], axis=-1, keepdims=True))
           - jnp.exp(jnp.sum(lp[2:3] * lp[3:4], axis=-1, keepdims=True)) + lam_init)
    ts = tq // n_sub
    lane = lax.broadcasted_iota(jnp.int32, (ts, LANES), 1)

    scored = []
    for a in range(n_sub):
        qa = q[a * ts:(a + 1) * ts]
        q2 = jnp.concatenate([jnp.where(lane < A_QK_DIM, qa, 0.0),
                              jnp.where(lane >= A_QK_DIM, qa, 0.0)], axis=0).astype(BF16)
        s = jnp.dot(q2, k_scr[...], preferred_element_type=F32)
        mx = jnp.max(s, axis=-1, keepdims=True)
        sc = None
        if has_ctx:
            sc = jnp.dot(q2, ck_scr[...], preferred_element_type=F32)
            mx = jnp.maximum(mx, jnp.max(sc, axis=-1, keepdims=True))
        scored.append((s, sc, mx))
    for a, (s, sc, mx) in enumerate(scored):
        e = jnp.exp(s - mx)
        l = jnp.sum(e, axis=-1, keepdims=True)
        if has_ctx:
            ec = jnp.exp(sc - mx)
            l = l + jnp.sum(ec, axis=-1, keepdims=True)
        r = 1.0 / l
        r0 = r[:ts]
        r1 = lam * r[ts:]
        d = jnp.dot((e[:ts] * r0 - e[ts:] * r1).astype(BF16), v_scr[...],
                    preferred_element_type=F32)
        if has_ctx:
            d = d + jnp.dot((ec[:ts] * r0 - ec[ts:] * r1).astype(BF16), cv_scr[...],
                            preferred_element_type=F32)
        y = d * lax.rsqrt(jnp.mean(d * d, axis=-1, keepdims=True) + EPS)
        o_ref[a * ts:(a + 1) * ts, :] = ((y * g_ref[...]) * (1.0 - lam_init)).astype(o_ref.dtype)


def diff_attention_call(proj, tok0, b, t, lam_p, subln_g, lam_init, ctx=None, rope=None, tq=256,
                        n_sub=2):
    tq = min(tq, t)
    nq = t // tq
    q0 = tok0 // tq
    k0 = tok0 // t
    has_ctx = ctx is not None
    in_specs = [pl.BlockSpec((tq, LANES), lambda bi, h, qi: (q0 + bi * nq + qi, h)),
                pl.BlockSpec((t, LANES), lambda bi, h, qi: (k0 + bi, A_HEADS + h)),
                pl.BlockSpec((t, LANES), lambda bi, h, qi: (k0 + bi, 2 * A_HEADS + h))]
    args = [proj, proj, proj]
    if has_ctx:
        ck, cv = ctx
        lc = ck.shape[1]
        cos, sin = rope
        in_specs += [pl.BlockSpec((None, lc, LANES), lambda bi, h, qi: (bi, 0, h)),
                     pl.BlockSpec((None, lc, LANES), lambda bi, h, qi: (bi, 0, h)),
                     pl.BlockSpec((t, LANES), lambda bi, h, qi: (0, 0)),
                     pl.BlockSpec((t, LANES), lambda bi, h, qi: (0, 0))]
        args += [ck, cv, cos, sin]
    in_specs += [pl.BlockSpec((4, A_QK_DIM), lambda bi, h, qi: (0, 0)),
                 pl.BlockSpec((1, A_V_DIM), lambda bi, h, qi: (0, 0))]
    args += [lam_p, subln_g.reshape(1, A_V_DIM)]
    scratch = [pltpu.VMEM((LANES, t), BF16), pltpu.VMEM((t, LANES), BF16)]
    if has_ctx:
        scratch += [pltpu.VMEM((LANES, lc), BF16), pltpu.VMEM((lc, LANES), BF16)]
    return pl.pallas_call(
        functools.partial(_diff_attn_kernel, has_ctx=has_ctx, lam_init=lam_init, tq=tq, n_sub=n_sub),
        grid=(b, A_HEADS, nq),
        in_specs=in_specs,
        out_specs=pl.BlockSpec((tq, LANES), lambda bi, h, qi: (bi * nq + qi, h)),
        out_shape=jax.ShapeDtypeStruct((b * t, A_WIDTH), BF16),
        scratch_shapes=scratch,
        compiler_params=_cparams(("arbitrary", "arbitrary", "arbitrary")),
        name="diff_attention",
    )(*args)


def rope_tables(t):
    tok = jnp.arange(t)
    row = (tok // GRID_W).astype(F32)
    col = (tok % GRID_W).astype(F32)
    n_freq = A_QK_DIM // 4
    inv = ROPE_BASE ** (-jnp.arange(n_freq, dtype=F32) / n_freq)
    ang_r = row[:, None] * inv
    ang_c = col[:, None] * inv
    cos64 = jnp.concatenate([jnp.cos(ang_r), jnp.cos(ang_r), jnp.cos(ang_c), jnp.cos(ang_c)], axis=1)
    sin64 = jnp.concatenate([-jnp.sin(ang_r), jnp.sin(ang_r), -jnp.sin(ang_c), jnp.sin(ang_c)], axis=1)
    return jnp.tile(cos64, (1, 2)), jnp.tile(sin64, (1, 2))


def _attn_kernel(q_ref, k_ref, v_ref, o_ref):
    q = q_ref[...].astype(BF16)
    k = k_ref[...].astype(BF16)
    s = lax.dot_general(q, k, NT_DIMS, preferred_element_type=F32) * (C_HEAD_DIM ** -0.5)
    mx = jnp.max(s, axis=-1, keepdims=True)
    e = jnp.exp(s - mx)
    l = jnp.sum(e, axis=-1, keepdims=True)
    o = jnp.dot(e.astype(BF16), v_ref[...].astype(BF16), preferred_element_type=F32)
    o_ref[...] = (o / l).astype(o_ref.dtype)


def attention_call(proj, b, t):
    return pl.pallas_call(
        _attn_kernel,
        grid=(b, C_HEADS),
        in_specs=[pl.BlockSpec((t, LANES), lambda bi, h: (bi, h)),
                  pl.BlockSpec((t, LANES), lambda bi, h: (bi, C_HEADS + h)),
                  pl.BlockSpec((t, LANES), lambda bi, h: (bi, 2 * C_HEADS + h))],
        out_specs=pl.BlockSpec((t, LANES), lambda bi, h: (bi, h)),
        out_shape=jax.ShapeDtypeStruct((b * t, C_WIDTH), BF16),
        compiler_params=_cparams(("arbitrary", "arbitrary")),
        name="ctx_attention",
    )(proj, proj, proj)


NA_QROWS = 8
NA_WROWS = 16
NA_CHUNK_ROWS = 4
NA_NCHUNK = NA_WROWS // NA_CHUNK_ROWS
NA_NSUB = 2


def _na_window_start(rb, rows):
    return np.clip(rb * NA_QROWS - NA_KH // 2, 0, rows - NA_WROWS)


def na_bias_tables(rpb, rows):
    n_rb = rows // NA_QROWS
    geos = []
    for rb in (0, 1, n_rb - 1):
        r0 = rb * NA_QROWS
        ws = _na_window_start(rb, rows)
        qr = r0 + np.arange(NA_QROWS)[:, None, None, None]
        qc = np.arange(GRID_W)[None, :, None, None]
        kr = ws + np.arange(NA_WROWS)[None, None, :, None]
        kc = np.arange(GRID_W)[None, None, None, :]
        rs = np.clip(qr - NA_KH // 2, 0, rows - NA_KH)
        cs = np.clip(qc - NA_KW // 2, 0, GRID_W - NA_KW)
        ok = (kr >= rs) & (kr < rs + NA_KH) & (kc >= cs) & (kc < cs + NA_KW)
        ri = np.clip(kr - qr + NA_KH - 1, 0, 2 * NA_KH - 2)
        ci = np.clip(kc - qc + NA_KW - 1, 0, 2 * NA_KW - 2)
        shape = (NA_QROWS, GRID_W, NA_WROWS, GRID_W)
        ok = np.broadcast_to(ok, shape).reshape(NA_QROWS * GRID_W, NA_WROWS * GRID_W)
        by_col = rpb[:, :, ci[0, :, 0, :]]
        tab = by_col[:, ri[:, 0, :, 0]]
        tab = tab.transpose(0, 1, 3, 2, 4).reshape((rpb.shape[0],) + ok.shape)
        geos.append(jnp.where(ok[None], tab, -jnp.inf))
    return jnp.stack(geos, axis=1).astype(F32)


def _na_kernel(*refs):
    q_ref = refs[0]
    k_refs = refs[1:1 + NA_NCHUNK]
    v_refs = refs[1 + NA_NCHUNK:1 + 2 * NA_NCHUNK]
    ck_ref, cv_ref, bias_ref, o_ref = refs[1 + 2 * NA_NCHUNK:]
    scale = C_HEAD_DIM ** -0.5
    kt = jnp.concatenate([r[...].T.astype(BF16) for r in k_refs], axis=1)
    v = jnp.concatenate([r[...].astype(BF16) for r in v_refs], axis=0)
    ckt = ck_ref[...].T.astype(BF16)
    cv = cv_ref[...].astype(BF16)
    ts = q_ref.shape[0] // NA_NSUB
    scored = []
    for a in range(NA_NSUB):
        rows = slice(a * ts, (a + 1) * ts)
        q = q_ref[rows, :].astype(BF16)
        s = jnp.dot(q, kt, preferred_element_type=F32) * scale + bias_ref[rows, :]
        sc = jnp.dot(q, ckt, preferred_element_type=F32) * scale
        mx = jnp.maximum(jnp.max(s, axis=-1, keepdims=True), jnp.max(sc, axis=-1, keepdims=True))
        scored.append((s, sc, mx))
    for a, (s, sc, mx) in enumerate(scored):
        e = jnp.exp(s - mx)
        ec = jnp.exp(sc - mx)
        l = jnp.sum(e, axis=-1, keepdims=True) + jnp.sum(ec, axis=-1, keepdims=True)
        o = (jnp.dot(e.astype(BF16), v, preferred_element_type=F32)
             + jnp.dot(ec.astype(BF16), cv, preferred_element_type=F32))
        o_ref[a * ts:(a + 1) * ts, :] = (o / l).astype(o_ref.dtype)


def na_call(proj, tok0, b, t, ck, cv, bias):
    rows = t // GRID_W
    n_rb = rows // NA_QROWS
    qtok = NA_QROWS * GRID_W
    ctok = NA_CHUNK_ROWS * GRID_W
    lc = ck.shape[1]
    q0 = tok0 // qtok
    c0 = tok0 // ctok
    cps = t // ctok

    def wchunk(rb):
        return jnp.clip(rb * (NA_QROWS // NA_CHUNK_ROWS) - 1, 0, (rows - NA_WROWS) // NA_CHUNK_ROWS)

    def geo(rb):
        return jnp.where(rb == 0, 0, jnp.where(rb == n_rb - 1, 2, 1))

    def kspec(c, colblk):
        return pl.BlockSpec((ctok, LANES),
                            lambda h, rb, bi: (c0 + bi * cps + wchunk(rb) + c, colblk + h))

    in_specs = ([pl.BlockSpec((qtok, LANES), lambda h, rb, bi: (q0 + bi * n_rb + rb, h))]
                + [kspec(c, C_HEADS) for c in range(NA_NCHUNK)]
                + [kspec(c, 2 * C_HEADS) for c in range(NA_NCHUNK)]
                + [pl.BlockSpec((None, lc, LANES), lambda h, rb, bi: (bi, 0, h)),
                   pl.BlockSpec((None, lc, LANES), lambda h, rb, bi: (bi, 0, h)),
                   pl.BlockSpec((None, None, qtok, NA_WROWS * GRID_W),
                                lambda h, rb, bi: (h, geo(rb), 0, 0))])
    return pl.pallas_call(
        _na_kernel,
        grid=(C_HEADS, n_rb, b),
        in_specs=in_specs,
        out_specs=pl.BlockSpec((qtok, LANES), lambda h, rb, bi: (bi * n_rb + rb, h)),
        out_shape=jax.ShapeDtypeStruct((b * t, C_WIDTH), BF16),
        compiler_params=_cparams(("arbitrary", "arbitrary", "arbitrary")),
        name="neighbourhood_attention",
    )(*([proj] * (1 + 2 * NA_NCHUNK)), ck, cv, bias)


LRU_HALO = SUBLANES


def _expm1(x):
    u = jnp.exp(x)
    return jnp.where(u == 1.0, x, (u - 1.0) * x / jnp.log(u))


def _lru_gates_kernel(x_ref, prev_ref, next_ref, cw_ref, cb_ref, wa_ref, ba_ref, wi_ref, bi_ref,
                      lam_ref, af_ref, bf_ref, ab_ref, bb_ref, *, tt, n_prompt, t_p, t_s):
    i = pl.program_id(0)
    s = i * tt
    in_prompt = s < n_prompt
    pos = jnp.where(in_prompt, s % t_p, (s - n_prompt) % t_s)
    seq_len = jnp.where(in_prompt, t_p, t_s)
    is_start = pos == 0
    is_end = pos + tt == seq_len
    x = x_ref[...]
    prev = jnp.where(is_start, 0.0, prev_ref[LRU_HALO - 2:LRU_HALO, :])
    nxt = jnp.where(is_end, 0.0, next_ref[0:1, :])
    xp = jnp.concatenate([prev, x, nxt], axis=0)
    cw = cw_ref[...]
    xc = cb_ref[...] + xp[0:tt] * cw[0:1]
    for j in range(1, CONV_W):
        xc = xc + xp[j:j + tt] * cw[j:j + 1]
    outs = ((af_ref, bf_ref), (ab_ref, bb_ref))
    for n in range(LRU_BLOCKS):
        sl = slice(n * LRU_BLOCK, (n + 1) * LRU_BLOCK)
        xb = xc[:, sl]
        xb16 = xb.astype(BF16)
        for dr in range(2):
            r = _sigmoid(jnp.dot(xb16, wa_ref[dr, n], preferred_element_type=F32) + ba_ref[dr:dr + 1, sl])
            g = _sigmoid(jnp.dot(xb16, wi_ref[dr, n], preferred_element_type=F32) + bi_ref[dr:dr + 1, sl])
            z = -lam_ref[dr:dr + 1, sl]
            softplus = jnp.maximum(z, 0.0) + jnp.log(1.0 + jnp.exp(-jnp.abs(z)))
            log_a = -LRU_C * r * softplus
            a_out, b_out = outs[dr]
            a_out[:, sl] = jnp.exp(log_a)
            b_out[:, sl] = jnp.sqrt(-_expm1(2.0 * log_a)) * g * xb


def lru_gates(proj, conv_w, conv_b, w_a, b_a, w_i, b_i, lam, *, n_prompt, t_p, t_s, tt=256):
    n = proj.shape[0]
    xcol = (3 * A_WIDTH) // LRU_WIDTH
    hb = tt // LRU_HALO
    nh = n // LRU_HALO
    full = lambda shape: pl.BlockSpec(shape, lambda i: (0,) * len(shape))
    out = jax.ShapeDtypeStruct((n, LRU_WIDTH), F32)
    ospec = pl.BlockSpec((tt, LRU_WIDTH), lambda i: (i, 0))
    return pl.pallas_call(
        functools.partial(_lru_gates_kernel, tt=tt, n_prompt=n_prompt, t_p=t_p, t_s=t_s),
        grid=(n // tt,),
        in_specs=[pl.BlockSpec((tt, LRU_WIDTH), lambda i: (i, xcol)),
                  pl.BlockSpec((LRU_HALO, LRU_WIDTH), lambda i: (jnp.maximum(i * hb - 1, 0), xcol)),
                  pl.BlockSpec((LRU_HALO, LRU_WIDTH), lambda i: (jnp.minimum((i + 1) * hb, nh - 1), xcol)),
                  full((CONV_W, LRU_WIDTH)), full((1, LRU_WIDTH)),
                  full((2, LRU_BLOCKS, LRU_BLOCK, LRU_BLOCK)), full((2, LRU_WIDTH)),
                  full((2, LRU_BLOCKS, LRU_BLOCK, LRU_BLOCK)), full((2, LRU_WIDTH)),
                  full((2, LRU_WIDTH))],
        out_specs=[ospec] * 4,
        out_shape=[out] * 4,
        compiler_params=_cparams(("arbitrary",)),
        name="lru_gates",
    )(proj, proj, proj, conv_w, conv_b.reshape(1, LRU_WIDTH), w_a.astype(BF16), b_a,
      w_i.astype(BF16), b_i, lam)


def _lru_scan_kernel(*refs, tt, reverse, combine):
    if combine:
        a_ref, b_ref, h0_ref, hf_ref, g_ref, o_ref, fin_ref, h_scr = refs
    else:
        a_ref, b_ref, h0_ref, o_ref, fin_ref, h_scr = refs
    ti = pl.program_id(1)

    @pl.when(ti == 0)
    def _():
        h_scr[...] = h0_ref[...]

    def step(j, h):
        t = tt - 1 - j if reverse else j
        h = a_ref[t] * h + b_ref[t]
        if combine:
            o_ref[t] = ((hf_ref[t] + h) * _gelu(g_ref[t])).astype(o_ref.dtype)
        else:
            o_ref[t] = h
        return h

    h = lax.fori_loop(0, tt, step, h_scr[...], unroll=8)
    h_scr[...] = h
    fin_ref[...] = h


def lru_scan(a, b, h0, tok0, nb, t, *, reverse, hf=None, proj=None, tt=256):
    tt = min(tt, t)
    nt = t // tt
    r0 = tok0 // tt
    combine = hf is not None
    vreg = (SUBLANES, LANES)
    view = lambda z: z.reshape(z.shape[0], LRU_WIDTH // LANES, LANES)

    def tmap(bi, ti):
        return nt - 1 - ti if reverse else ti

    in_specs = [pl.BlockSpec((tt,) + vreg, lambda bi, ti: (r0 + bi * nt + tmap(bi, ti), 0, 0)),
                pl.BlockSpec((tt,) + vreg, lambda bi, ti: (r0 + bi * nt + tmap(bi, ti), 0, 0)),
                pl.BlockSpec((None,) + vreg, lambda bi, ti: (bi, 0, 0))]
    args = [view(a), view(b), view(h0)]
    if combine:
        gcol = (3 * A_WIDTH + LRU_WIDTH) // LRU_WIDTH
        in_specs += [pl.BlockSpec((tt,) + vreg, lambda bi, ti: (bi * nt + tmap(bi, ti), 0, 0)),
                     pl.BlockSpec((tt,) + vreg, lambda bi, ti: (r0 + bi * nt + tmap(bi, ti), gcol, 0))]
        args += [view(hf), proj.reshape(proj.shape[0], EVEN_IN // LANES, LANES)]
    out, fin = pl.pallas_call(
        functools.partial(_lru_scan_kernel, tt=tt, reverse=reverse, combine=combine),
        grid=(nb, nt),
        in_specs=in_specs,
        out_specs=[pl.BlockSpec((tt,) + vreg, lambda bi, ti: (bi * nt + tmap(bi, ti), 0, 0)),
                   pl.BlockSpec((None,) + vreg, lambda bi, ti: (bi, 0, 0))],
        out_shape=[jax.ShapeDtypeStruct((nb * t,) + vreg, F32),
                   jax.ShapeDtypeStruct((nb,) + vreg, F32)],
        scratch_shapes=[pltpu.VMEM(vreg, F32)],
        compiler_params=_cparams(("arbitrary", "arbitrary")),
        name="lru_scan_bwd" if reverse else "lru_scan_fwd",
    )(*args)
    return out.reshape(nb * t, LRU_WIDTH), fin.reshape(nb, LRU_WIDTH)


def _topk_rows(s_list, k, payloads=None):
    n = s_list[0].shape[0]
    iota = lax.broadcasted_iota(jnp.int32, s_list[0].shape, 0).astype(F32)
    s_list = list(s_list)
    vals = [[] for _ in s_list]
    idxs = [[] for _ in s_list]
    for _ in range(k):
        for j, s in enumerate(s_list):
            m = jnp.max(s, axis=0, keepdims=True)
            am = jnp.min(jnp.where(s == m, iota, float(n)), axis=0, keepdims=True)
            sel = iota == am
            vals[j].append(m)
            if payloads is None:
                idxs[j].append(am)
            else:
                idxs[j].append(jnp.sum(jnp.where(sel, payloads[j], 0.0), axis=0, keepdims=True))
            s_list[j] = jnp.where(sel, -jnp.inf, s)
    return [(jnp.concatenate(v, axis=0), jnp.concatenate(i, axis=0)) for v, i in zip(vals, idxs)]


def _peer_route_kernel(q_ref, sk_ref, idx_ref, gate_ref, *, tm):
    half = PEER_QDIM // 2
    n_chunk = tm // LANES
    scores = []
    for c in range(n_chunk):
        q = q_ref[c * LANES:(c + 1) * LANES, :].astype(BF16)
        for p in range(2):
            scores.append(lax.dot_general(sk_ref[p], q[:, p * half:(p + 1) * half], NT_DIMS,
                                          preferred_element_type=F32))
    tops = _topk_rows(scores, PEER_TOPK)
    nb = [PEER_TOPK // (a + 1) for a in range(PEER_TOPK)]
    pad = -sum(nb) % SUBLANES
    cands, cidxs = [], []
    for c in range(n_chunk):
        (s1, i1), (s2, i2) = tops[2 * c], tops[2 * c + 1]
        cands.append(jnp.concatenate([s1[a:a + 1] + s2[:nb[a]] for a in range(PEER_TOPK)]
                                     + [jnp.full((pad, LANES), -jnp.inf, F32)], axis=0))
        cidxs.append(jnp.concatenate(
            [i1[a:a + 1] * float(PEER_NKEYS) + i2[:nb[a]] for a in range(PEER_TOPK)]
            + [jnp.zeros((pad, LANES), F32)], axis=0))
    best = _topk_rows(cands, PEER_TOPK, payloads=cidxs)
    for c, (bs, bidx) in enumerate(best):
        e = jnp.exp(bs - bs[0:1])
        gates = e / jnp.sum(e, axis=0, keepdims=True)
        idx_ref[:, c * LANES:(c + 1) * LANES] = bidx.astype(jnp.int32)
        gate_ref[:, c * LANES:(c + 1) * LANES] = gates


def peer_route(q, sub_keys, tm=512):
    n = q.shape[0]
    half = PEER_QDIM // 2
    return pl.pallas_call(
        functools.partial(_peer_route_kernel, tm=tm),
        grid=(n // tm, PEER_HEADS),
        in_specs=[pl.BlockSpec((tm, PEER_QDIM), lambda i, h: (i, h)),
                  pl.BlockSpec((None, 2, PEER_NKEYS, half), lambda i, h: (h, 0, 0, 0))],
        out_specs=[pl.BlockSpec((None, PEER_TOPK, tm), lambda i, h: (h, 0, i))] * 2,
        out_shape=[jax.ShapeDtypeStruct((PEER_HEADS, PEER_TOPK, n), jnp.int32),
                   jax.ShapeDtypeStruct((PEER_HEADS, PEER_TOPK, n), F32)],
        compiler_params=_cparams(("arbitrary", "arbitrary")),
        name="peer_route",
    )(q, sub_keys)


def _peer_gates_kernel(idx_ref, gate_ref, g_ref):
    idx = idx_ref[...]
    gate = gate_ref[...]
    tg, _, hk = idx.shape
    i1 = (idx // PEER_NKEYS).astype(F32).astype(BF16)
    i2 = (idx % PEER_NKEYS).astype(F32).astype(BF16)
    iota = lax.broadcasted_iota(jnp.int32, (1, PEER_NKEYS, hk), 1).astype(F32).astype(BF16)
    zero = jnp.zeros((), BF16)
    a = jnp.where(i1 == iota, gate.astype(BF16), zero)
    bsel = jnp.where(i2 == iota, jnp.ones((), BF16), zero)
    g3 = lax.dot_general(a, bsel, (((2,), (2,)), ((0,), (0,))), preferred_element_type=F32)
    g_ref[...] = pltpu.einshape("nij->inj", g3).astype(g_ref.dtype)


def peer_gate_rows(idx, gates, tg=64):
    n, hk = idx.shape
    return pl.pallas_call(
        _peer_gates_kernel,
        grid=(n // tg,),
        in_specs=[pl.BlockSpec((tg, 1, hk), lambda i: (i, 0, 0))] * 2,
        out_specs=pl.BlockSpec((PEER_NKEYS, tg, PEER_NKEYS), lambda i: (0, i, 0)),
        out_shape=jax.ShapeDtypeStruct((PEER_NKEYS, n, PEER_NKEYS), BF16),
        compiler_params=_cparams(("arbitrary",)),
        name="peer_gate_rows",
    )(idx.reshape(n, 1, hk), gates.reshape(n, 1, hk))


def _peer_dense_kernel(h_ref, ut_ref, v_ref, gr_ref, x_ref, mg_ref, fg_ref, o_ref, *, final_norm,
                       row_split):
    e = pl.program_id(1)

    @pl.when(e == 0)
    def _():
        o_ref[...] = jnp.zeros_like(o_ref)

    tr = h_ref.shape[0] // row_split
    scores = [jnp.dot(h_ref[r * tr:(r + 1) * tr, :], ut_ref[...], preferred_element_type=F32)
              for r in range(row_split)]
    for r, s in enumerate(scores):
        rows = slice(r * tr, (r + 1) * tr)
        act = jnp.concatenate(
            [(_gelu(s[:, j * PEER_NKEYS:(j + 1) * PEER_NKEYS])
              * gr_ref[j, rows, :].astype(F32)).astype(BF16)
             for j in range(gr_ref.shape[0])], axis=1)
        o_ref[rows, :] += jnp.dot(act, v_ref[...], preferred_element_type=F32)

    @pl.when(e == pl.num_programs(1) - 1)
    def _():
        y = x_ref[...] + mg_ref[...] * o_ref[...]
        if final_norm:
            y = y * lax.rsqrt(jnp.mean(y * y, axis=-1, keepdims=True) + EPS) * fg_ref[...]
        o_ref[...] = y


def peer_dense(h, ut16, v16, grows, x, mod, gate_idx, final_g, *, n_prompt, t_s, final_norm,
               tm=512, te=1024, row_split=2):
    n, d = x.shape
    ne = v16.shape[0]
    row = functools.partial(_mod_row, tm=tm, n_prompt=n_prompt, t_s=t_s)
    return pl.pallas_call(
        functools.partial(_peer_dense_kernel, final_norm=final_norm, row_split=row_split),
        grid=(n // tm, ne // te),
        in_specs=[pl.BlockSpec((tm, d), lambda i, e: (i, 0)),
                  pl.BlockSpec((d, te), lambda i, e: (0, e)),
                  pl.BlockSpec((te, d), lambda i, e: (e, 0)),
                  pl.BlockSpec((te // PEER_NKEYS, tm, PEER_NKEYS), lambda i, e: (e, i, 0)),
                  pl.BlockSpec((tm, d), lambda i, e: (i, 0)),
                  pl.BlockSpec((None, None, 1, d), lambda i, e: (row(i), gate_idx, 0, 0)),
                  pl.BlockSpec((1, d), lambda i, e: (0, 0))],
        out_specs=pl.BlockSpec((tm, d), lambda i, e: (i, 0)),
        out_shape=jax.ShapeDtypeStruct((n, d), F32),
        compiler_params=_cparams(("arbitrary", "arbitrary")),
        name="peer_dense",
    )(h, ut16, v16, grows, x, mod, final_g.reshape(1, d))


def peer_block(x, norm_g, mod, w_q16, sk16, ut16, v16, final_g, *, n_prompt, t_s, final_norm):
    n = x.shape[0]
    q, h = norm_matmul(x, norm_g, mod, 3, 4, w_q16, n_prompt=n_prompt, t_s=t_s, emit_h=True)
    idx, gates = peer_route(q, sk16)
    hk = PEER_HEADS * PEER_TOPK
    idx = idx.reshape(hk, n).T
    gates = gates.reshape(hk, n).T
    grows = peer_gate_rows(idx, gates)
    return peer_dense(h, ut16, v16, grows, x, mod, 5, final_g, n_prompt=n_prompt, t_s=t_s,
                      final_norm=final_norm)


def kernel(x_prompt, x_sample, cache_a_k, cache_a_v, state_lru, cache_c_k, cache_c_v, c, c_ctx, w_mod, b_mod, norm1_g, norm2_g, final_norm_g, even_w_in, even_w_out, a_lambda, a_subln_g, lru_conv_w, lru_conv_b, lru_w_a, lru_b_a, lru_w_i, lru_b_i, lru_lambda, odd_w_in, odd_w_out, na_rpb, peer_w_q, peer_sub_keys, peer_u, peer_v):
    b_p, t_p, d = x_prompt.shape
    b_s, t_s, _ = x_sample.shape
    depth = w_mod.shape[0]
    n_p = b_p * t_p
    n_s = b_s * t_s
    lc = cache_a_k.shape[2]
    tok = dict(n_prompt=n_p, t_s=t_s)

    x = jnp.concatenate([x_prompt.reshape(n_p, d), x_sample.reshape(n_s, d)], axis=0)
    n_rows = -(-(1 + b_s) // SUBLANES) * SUBLANES
    cond = jnp.concatenate([c_ctx[None], c, jnp.zeros((n_rows - 1 - b_s, d), F32)], axis=0)
    mod_all = modulation_all(cond, w_mod, b_mod).reshape(depth, n_rows, 6, 1, d)
    rope = rope_tables(t_s)

    new_ak, new_av, new_lru, new_ck, new_cv = [], [], [], [], []
    for l in range(depth):
        mod = mod_all[l]
        if l % 2 == 0:
            e = l // 2
            lam_init = 0.8 - 0.6 * math.exp(-0.3 * l)
            proj = norm_matmul(x, norm1_g[l], mod, 0, 1, even_w_in[e].astype(BF16), **tok)
            ctx = (cache_a_k[:, e].reshape(b_s, lc, A_WIDTH), cache_a_v[:, e].reshape(b_s, lc, A_WIDTH))
            oa_p = diff_attention_call(proj, 0, b_p, t_p, a_lambda[e], a_subln_g[e], lam_init)
            oa_s = diff_attention_call(proj, n_p, b_s, t_s, a_lambda[e], a_subln_g[e], lam_init,
                                       ctx=ctx, rope=rope)
            a_f, b_f, a_b, b_b = lru_gates(proj, lru_conv_w[e], lru_conv_b[e], lru_w_a[e], lru_b_a[e],
                                           lru_w_i[e], lru_b_i[e], lru_lambda[e],
                                           n_prompt=n_p, t_p=t_p, t_s=t_s)
            zeros_p = jnp.zeros((b_p, LRU_WIDTH), F32)
            hf_p, fin_f = lru_scan(a_f, b_f, zeros_p, 0, b_p, t_p, reverse=False)
            ob_p, fin_b = lru_scan(a_b, b_b, zeros_p, 0, b_p, t_p, reverse=True, hf=hf_p, proj=proj)
            hf_s, _ = lru_scan(a_f, b_f, state_lru[:, e, 0], n_p, b_s, t_s, reverse=False)
            ob_s, _ = lru_scan(a_b, b_b, state_lru[:, e, 1], n_p, b_s, t_s, reverse=True,
                               hf=hf_s, proj=proj)
            w_out = even_w_out[e].astype(BF16)
            x = matmul_res([(oa_p, oa_s), (ob_p, ob_s)], [w_out[:A_WIDTH], w_out[A_WIDTH:]],
                           x, mod, 2, **tok)
            new_ak.append(proj[:n_p, A_WIDTH:2 * A_WIDTH].reshape(b_p, t_p, A_HEADS, 2 * A_QK_DIM))
            new_av.append(proj[:n_p, 2 * A_WIDTH:3 * A_WIDTH].reshape(b_p, t_p, A_HEADS, A_V_DIM))
            new_lru.append(jnp.stack([fin_f, fin_b], axis=1))
        else:
            o = l // 2
            proj = norm_matmul(x, norm1_g[l], mod, 0, 1, odd_w_in[o].astype(BF16), **tok)
            oc_p = attention_call(proj, b_p, t_p)
            bias = na_bias_tables(na_rpb[o], t_s // GRID_W)
            oc_s = na_call(proj, n_p, b_s, t_s, cache_c_k[:, o].reshape(b_s, lc, C_WIDTH),
                           cache_c_v[:, o].reshape(b_s, lc, C_WIDTH), bias)
            x = matmul_res([(oc_p, oc_s)], [odd_w_out[o].astype(BF16)], x, mod, 2, **tok)
            new_ck.append(proj[:n_p, C_WIDTH:2 * C_WIDTH].reshape(b_p, t_p, C_HEADS, C_HEAD_DIM))
            new_cv.append(proj[:n_p, 2 * C_WIDTH:3 * C_WIDTH].reshape(b_p, t_p, C_HEADS, C_HEAD_DIM))
        x = peer_block(x, norm2_g[l], mod, peer_w_q[l].astype(BF16), peer_sub_keys[l].astype(BF16),
                       peer_u[l].astype(BF16).T, peer_v[l].astype(BF16), final_norm_g,
                       final_norm=(l == depth - 1), **tok)
    y_prompt = x[:n_p].reshape(b_p, t_p, d)
    y_sample = x[n_p:].reshape(b_s, t_s, d)
    return (y_prompt, y_sample, jnp.stack(new_ak, axis=1), jnp.stack(new_av, axis=1),
            jnp.stack(new_lru, axis=1), jnp.stack(new_ck, axis=1), jnp.stack(new_cv, axis=1))
```

```python
import functools
import math

import numpy as np
import jax
import jax.numpy as jnp
from jax import lax
from jax.experimental import pallas as pl
from jax.experimental.pallas import tpu as pltpu

F32 = jnp.float32
BF16 = jnp.bfloat16

D_MODEL = 2048
GRID_W = 64
EPS = 1e-6
ROPE_BASE = 10000.0
A_HEADS = 8
A_QK_DIM = 64
A_V_DIM = 128
A_WIDTH = A_HEADS * A_V_DIM
LRU_WIDTH = 1024
LRU_BLOCKS = 8
LRU_BLOCK = LRU_WIDTH // LRU_BLOCKS
CONV_W = 4
LRU_C = 8.0
EVEN_IN = 3 * A_WIDTH + 2 * LRU_WIDTH
C_HEADS = 16
C_HEAD_DIM = 128
C_WIDTH = C_HEADS * C_HEAD_DIM
NA_KH = 8
NA_KW = 16
PEER_HEADS = 8
PEER_NKEYS = 128
PEER_N = PEER_NKEYS * PEER_NKEYS
PEER_QDIM = 256
PEER_TOPK = 16

LANES = 128
SUBLANES = 8
VMEM_LIMIT = 56 * 1024 * 1024

NT_DIMS = (((1,), (1,)), ((), ()))


def _cparams(sem):
    return pltpu.CompilerParams(dimension_semantics=sem, vmem_limit_bytes=VMEM_LIMIT)


def _gelu(x):
    c = math.sqrt(2.0 / math.pi)
    return 0.5 * x * (1.0 + jnp.tanh(c * (x + 0.044715 * (x * x * x))))


def _sigmoid(x):
    return 1.0 / (1.0 + jnp.exp(-x))


def _mod_kernel(c_ref, w_ref, b_ref, o_ref):
    c = c_ref[...]
    s = c * _sigmoid(c)
    o_ref[0] = jnp.dot(s.astype(BF16), w_ref[0].astype(BF16),
                       preferred_element_type=F32) + b_ref[0]


def modulation_all(cond, w_mod, b_mod):
    depth, d, n6 = w_mod.shape
    r = cond.shape[0]
    tn = 768
    return pl.pallas_call(
        _mod_kernel,
        grid=(depth, n6 // tn),
        in_specs=[pl.BlockSpec((r, d), lambda l, j: (0, 0)),
                  pl.BlockSpec((1, d, tn), lambda l, j: (l, 0, j)),
                  pl.BlockSpec((1, 1, tn), lambda l, j: (l, 0, j))],
        out_specs=pl.BlockSpec((1, r, tn), lambda l, j: (l, 0, j)),
        out_shape=jax.ShapeDtypeStruct((depth, r, n6), F32),
        compiler_params=_cparams(("arbitrary", "arbitrary")),
        name="modulation",
    )(cond, w_mod, b_mod.reshape(depth, 1, n6))


def _mod_row(i, tm, n_prompt, t_s):
    s = i * tm
    return jnp.where(s < n_prompt, 0, 1 + (s - n_prompt) // t_s)


def _norm_matmul_kernel(x_ref, g_ref, sh_ref, sc_ref, w_ref, o_ref, *rest, emit_h):
    if emit_h:
        h_out_ref, h_scr = rest
    else:
        (h_scr,) = rest

    @pl.when(pl.program_id(1) == 0)
    def _():
        x = x_ref[...]
        y = x * lax.rsqrt(jnp.mean(x * x, axis=-1, keepdims=True) + EPS)
        h = (y * g_ref[...]) * (1.0 + sc_ref[...]) + sh_ref[...]
        h_scr[...] = h.astype(BF16)
        if emit_h:
            h_out_ref[...] = h.astype(BF16)

    o_ref[...] = jnp.dot(h_scr[...], w_ref[...], preferred_element_type=F32).astype(o_ref.dtype)


def norm_matmul(x, g, mod, shift_idx, scale_idx, w, *, n_prompt, t_s, emit_h=False,
                tm=1024, tn=1024, out_dtype=F32):
    n, d = x.shape
    nout = w.shape[1]
    row = functools.partial(_mod_row, tm=tm, n_prompt=n_prompt, t_s=t_s)
    out_shape = [jax.ShapeDtypeStruct((n, nout), out_dtype)]
    out_specs = [pl.BlockSpec((tm, tn), lambda i, j: (i, j))]
    if emit_h:
        out_shape.append(jax.ShapeDtypeStruct((n, d), BF16))
        out_specs.append(pl.BlockSpec((tm, d), lambda i, j: (i, 0)))
    res = pl.pallas_call(
        functools.partial(_norm_matmul_kernel, emit_h=emit_h),
        grid=(n // tm, nout // tn),
        in_specs=[pl.BlockSpec((tm, d), lambda i, j: (i, 0)),
                  pl.BlockSpec((1, d), lambda i, j: (0, 0)),
                  pl.BlockSpec((None, None, 1, d), lambda i, j: (row(i), shift_idx, 0, 0)),
                  pl.BlockSpec((None, None, 1, d), lambda i, j: (row(i), scale_idx, 0, 0)),
                  pl.BlockSpec((d, tn), lambda i, j: (0, j))],
        out_specs=out_specs,
        out_shape=out_shape,
        scratch_shapes=[pltpu.VMEM((tm, d), BF16)],
        compiler_params=_cparams(("arbitrary", "arbitrary")),
        name="norm_matmul",
    )(x, g.reshape(1, d), mod, mod, w)
    return res if emit_h else res[0]


def _matmul_res_kernel(*refs, n_in, np_tiles):
    ap_refs = refs[:n_in]
    as_refs = refs[n_in:2 * n_in]
    w_refs = refs[2 * n_in:3 * n_in]
    x_ref, gate_ref, o_ref = refs[3 * n_in:]

    def emit(a_refs):
        acc = None
        for a_ref, w_ref in zip(a_refs, w_refs):
            p = jnp.dot(a_ref[...].astype(BF16), w_ref[...], preferred_element_type=F32)
            acc = p if acc is None else acc + p
        o_ref[...] = x_ref[...] + gate_ref[...] * acc

    i = pl.program_id(0)
    pl.when(i < np_tiles)(lambda: emit(ap_refs))
    pl.when(i >= np_tiles)(lambda: emit(as_refs))


def matmul_res(a_pairs, w_list, x, mod, gate_idx, *, n_prompt, t_s, tm=1024, tn=512):
    n, d = x.shape
    n_in = len(a_pairs)
    np_tiles = n_prompt // tm
    ns_tiles = n // tm - np_tiles
    row = functools.partial(_mod_row, tm=tm, n_prompt=n_prompt, t_s=t_s)
    in_specs = ([pl.BlockSpec((tm, ap.shape[1]), lambda i, j: (jnp.minimum(i, np_tiles - 1), 0))
                 for ap, _ in a_pairs]
                + [pl.BlockSpec((tm, a_s.shape[1]),
                                lambda i, j: (jnp.clip(i - np_tiles, 0, ns_tiles - 1), 0))
                   for _, a_s in a_pairs]
                + [pl.BlockSpec((w.shape[0], tn), lambda i, j: (0, j)) for w in w_list]
                + [pl.BlockSpec((tm, tn), lambda i, j: (i, j)),
                   pl.BlockSpec((None, None, 1, tn), lambda i, j: (row(i), gate_idx, 0, j))])
    return pl.pallas_call(
        functools.partial(_matmul_res_kernel, n_in=n_in, np_tiles=np_tiles),
        grid=(n // tm, d // tn),
        in_specs=in_specs,
        out_specs=pl.BlockSpec((tm, tn), lambda i, j: (i, j)),
        out_shape=jax.ShapeDtypeStruct((n, d), F32),
        compiler_params=_cparams(("arbitrary", "arbitrary")),
        name="matmul_res",
    )(*[ap for ap, _ in a_pairs], *[a_s for _, a_s in a_pairs], *w_list, x, mod)


def _rope(x, cos, sin_signed):
    lane = lax.broadcasted_iota(jnp.int32, x.shape, 1)
    first = (lane % 32) < 16
    partner = jnp.where(first, pltpu.roll(x, LANES - 16, axis=1), pltpu.roll(x, 16, axis=1))
    return x * cos + partner * sin_signed


def _diff_attn_kernel(*refs, has_ctx, lam_init, tq, n_sub):
    if has_ctx:
        (q_ref, k_ref, v_ref, ck_ref, cv_ref, cos_ref, sin_ref, lam_ref, g_ref,
         o_ref, k_scr, v_scr, ck_scr, cv_scr) = refs
    else:
        q_ref, k_ref, v_ref, lam_ref, g_ref, o_ref, k_scr, v_scr = refs
    qi = pl.program_id(2)

    @pl.when(qi == 0)
    def _():
        k = k_ref[...]
        if has_ctx:
            k = _rope(k, cos_ref[...], sin_ref[...])
            ck_scr[...] = ck_ref[...].astype(BF16)
            cv_scr[...] = cv_ref[...].astype(BF16)
        k_scr[...] = k.astype(BF16)
        v_scr[...] = v_ref[...].astype(BF16)

    q = q_ref[...]
    if has_ctx:
        off = pl.multiple_of(qi * tq, tq)
        q = _rope(q, cos_ref[pl.ds(off, tq), :], sin_ref[pl.ds(off, tq), :])
    q = q * (A_QK_DIM ** -0.5)
    lp = lam_ref[...]
    lam = (jnp.exp(jnp.sum(lp[0:1] * lp[1:2], axis=-1, keepdims=True))
           - jnp.exp(jnp.sum(lp[2:3] * lp[3:4], axis=-1, keepdims=True)) + lam_init)
    ts = tq // n_sub
    lane = lax.broadcasted_iota(jnp.int32, (ts, LANES), 1)

    scored = []
    for a in range(n_sub):
        qa = q[a * ts:(a + 1) * ts]
        q2 = jnp.concatenate([jnp.where(lane < A_QK_DIM, qa, 0.0),
                              jnp.where(lane >= A_QK_DIM, qa, 0.0)], axis=0).astype(BF16)
        s = lax.dot_general(q2, k_scr[...], NT_DIMS, preferred_element_type=F32)
        mx = jnp.max(s, axis=-1, keepdims=True)
        sc = None
        if has_ctx:
            sc = lax.dot_general(q2, ck_scr[...], NT_DIMS, preferred_element_type=F32)
            mx = jnp.maximum(mx, jnp.max(sc, axis=-1, keepdims=True))
        scored.append((s, sc, mx))
    for a, (s, sc, mx) in enumerate(scored):
        e = jnp.exp(s - mx)
        l = jnp.sum(e, axis=-1, keepdims=True)
        if has_ctx:
            ec = jnp.exp(sc - mx)
            l = l + jnp.sum(ec, axis=-1, keepdims=True)
        r = 1.0 / l
        r0 = r[:ts]
        r1 = lam * r[ts:]
        d = jnp.dot((e[:ts] * r0 - e[ts:] * r1).astype(BF16), v_scr[...],
                    preferred_element_type=F32)
        if has_ctx:
            d = d + jnp.dot((ec[:ts] * r0 - ec[ts:] * r1).astype(BF16), cv_scr[...],
                            preferred_element_type=F32)
        y = d * lax.rsqrt(jnp.mean(d * d, axis=-1, keepdims=True) + EPS)
        o_ref[a * ts:(a + 1) * ts, :] = ((y * g_ref[...]) * (1.0 - lam_init)).astype(o_ref.dtype)


def diff_attention_call(proj, tok0, b, t, lam_p, subln_g, lam_init, ctx=None, rope=None, tq=512,
                        n_sub=4):
    tq = min(tq, t)
    nq = t // tq
    q0 = tok0 // tq
    k0 = tok0 // t
    has_ctx = ctx is not None
    in_specs = [pl.BlockSpec((tq, LANES), lambda bi, h, qi: (q0 + bi * nq + qi, h)),
                pl.BlockSpec((t, LANES), lambda bi, h, qi: (k0 + bi, A_HEADS + h)),
                pl.BlockSpec((t, LANES), lambda bi, h, qi: (k0 + bi, 2 * A_HEADS + h))]
    args = [proj, proj, proj]
    if has_ctx:
        ck, cv = ctx
        lc = ck.shape[1]
        cos, sin = rope
        in_specs += [pl.BlockSpec((None, lc, LANES), lambda bi, h, qi: (bi, 0, h)),
                     pl.BlockSpec((None, lc, LANES), lambda bi, h, qi: (bi, 0, h)),
                     pl.BlockSpec((t, LANES), lambda bi, h, qi: (0, 0)),
                     pl.BlockSpec((t, LANES), lambda bi, h, qi: (0, 0))]
        args += [ck, cv, cos, sin]
    in_specs += [pl.BlockSpec((4, A_QK_DIM), lambda bi, h, qi: (0, 0)),
                 pl.BlockSpec((1, A_V_DIM), lambda bi, h, qi: (0, 0))]
    args += [lam_p, subln_g.reshape(1, A_V_DIM)]
    scratch = [pltpu.VMEM((t, LANES), BF16), pltpu.VMEM((t, LANES), BF16)]
    if has_ctx:
        scratch += [pltpu.VMEM((lc, LANES), BF16), pltpu.VMEM((lc, LANES), BF16)]
    return pl.pallas_call(
        functools.partial(_diff_attn_kernel, has_ctx=has_ctx, lam_init=lam_init, tq=tq, n_sub=n_sub),
        grid=(b, A_HEADS, nq),
        in_specs=in_specs,
        out_specs=pl.BlockSpec((tq, LANES), lambda bi, h, qi: (bi * nq + qi, h)),
        out_shape=jax.ShapeDtypeStruct((b * t, A_WIDTH), BF16),
        scratch_shapes=scratch,
        compiler_params=_cparams(("arbitrary", "arbitrary", "arbitrary")),
        name="diff_attention",
    )(*args)


def rope_tables(t):
    tok = jnp.arange(t)
    row = (tok // GRID_W).astype(F32)
    col = (tok % GRID_W).astype(F32)
    n_freq = A_QK_DIM // 4
    inv = ROPE_BASE ** (-jnp.arange(n_freq, dtype=F32) / n_freq)
    ang_r = row[:, None] * inv
    ang_c = col[:, None] * inv
    cos64 = jnp.concatenate([jnp.cos(ang_r), jnp.cos(ang_r), jnp.cos(ang_c), jnp.cos(ang_c)], axis=1)
    sin64 = jnp.concatenate([-jnp.sin(ang_r), jnp.sin(ang_r), -jnp.sin(ang_c), jnp.sin(ang_c)], axis=1)
    return jnp.tile(cos64, (1, 2)), jnp.tile(sin64, (1, 2))


def _attn_kernel(q_ref, k_ref, v_ref, o_ref):
    q = q_ref[...].astype(BF16)
    k = k_ref[...].astype(BF16)
    s = lax.dot_general(q, k, NT_DIMS, preferred_element_type=F32) * (C_HEAD_DIM ** -0.5)
    mx = jnp.max(s, axis=-1, keepdims=True)
    e = jnp.exp(s - mx)
    l = jnp.sum(e, axis=-1, keepdims=True)
    o = jnp.dot(e.astype(BF16), v_ref[...].astype(BF16), preferred_element_type=F32)
    o_ref[...] = (o / l).astype(o_ref.dtype)


def attention_call(proj, b, t):
    return pl.pallas_call(
        _attn_kernel,
        grid=(b, C_HEADS),
        in_specs=[pl.BlockSpec((t, LANES), lambda bi, h: (bi, h)),
                  pl.BlockSpec((t, LANES), lambda bi, h: (bi, C_HEADS + h)),
                  pl.BlockSpec((t, LANES), lambda bi, h: (bi, 2 * C_HEADS + h))],
        out_specs=pl.BlockSpec((t, LANES), lambda bi, h: (bi, h)),
        out_shape=jax.ShapeDtypeStruct((b * t, C_WIDTH), BF16),
        compiler_params=_cparams(("arbitrary", "arbitrary")),
        name="ctx_attention",
    )(proj, proj, proj)


NA_QROWS = 8
NA_WROWS = 16
NA_CHUNK_ROWS = 4
NA_NCHUNK = NA_WROWS // NA_CHUNK_ROWS
NA_NSUB = 2


def _na_window_start(rb, rows):
    return np.clip(rb * NA_QROWS - NA_KH // 2, 0, rows - NA_WROWS)


def na_bias_tables(rpb, rows):
    n_rb = rows // NA_QROWS
    geos = []
    for rb in (0, 1, n_rb - 1):
        r0 = rb * NA_QROWS
        ws = _na_window_start(rb, rows)
        qr = r0 + np.arange(NA_QROWS)[:, None, None, None]
        qc = np.arange(GRID_W)[None, :, None, None]
        kr = ws + np.arange(NA_WROWS)[None, None, :, None]
        kc = np.arange(GRID_W)[None, None, None, :]
        rs = np.clip(qr - NA_KH // 2, 0, rows - NA_KH)
        cs = np.clip(qc - NA_KW // 2, 0, GRID_W - NA_KW)
        ok = (kr >= rs) & (kr < rs + NA_KH) & (kc >= cs) & (kc < cs + NA_KW)
        ri = np.clip(kr - qr + NA_KH - 1, 0, 2 * NA_KH - 2)
        ci = np.clip(kc - qc + NA_KW - 1, 0, 2 * NA_KW - 2)
        shape = (NA_QROWS, GRID_W, NA_WROWS, GRID_W)
        ok = np.broadcast_to(ok, shape).reshape(NA_QROWS * GRID_W, NA_WROWS * GRID_W)
        by_col = rpb[:, :, ci[0, :, 0, :]]
        tab = by_col[:, ri[:, 0, :, 0]]
        tab = tab.transpose(0, 1, 3, 2, 4).reshape((rpb.shape[0],) + ok.shape)
        geos.append(jnp.where(ok[None], tab, -jnp.inf))
    return jnp.stack(geos, axis=1).astype(F32)


def _na_kernel(*refs):
    q_ref = refs[0]
    k_refs = refs[1:1 + NA_NCHUNK]
    v_refs = refs[1 + NA_NCHUNK:1 + 2 * NA_NCHUNK]
    ck_ref, cv_ref, bias_ref, o_ref = refs[1 + 2 * NA_NCHUNK:]
    scale = C_HEAD_DIM ** -0.5
    k = jnp.concatenate([r[...].astype(BF16) for r in k_refs], axis=0)
    v = jnp.concatenate([r[...].astype(BF16) for r in v_refs], axis=0)
    ck = ck_ref[...].astype(BF16)
    cv = cv_ref[...].astype(BF16)
    ts = q_ref.shape[0] // NA_NSUB
    scored = []
    for a in range(NA_NSUB):
        rows = slice(a * ts, (a + 1) * ts)
        q = q_ref[rows, :].astype(BF16)
        s = lax.dot_general(q, k, NT_DIMS, preferred_element_type=F32) * scale + bias_ref[rows, :]
        sc = lax.dot_general(q, ck, NT_DIMS, preferred_element_type=F32) * scale
        mx = jnp.maximum(jnp.max(s, axis=-1, keepdims=True), jnp.max(sc, axis=-1, keepdims=True))
        scored.append((s, sc, mx))
    for a, (s, sc, mx) in enumerate(scored):
        e = jnp.exp(s - mx)
        ec = jnp.exp(sc - mx)
        l = jnp.sum(e, axis=-1, keepdims=True) + jnp.sum(ec, axis=-1, keepdims=True)
        o = (jnp.dot(e.astype(BF16), v, preferred_element_type=F32)
             + jnp.dot(ec.astype(BF16), cv, preferred_element_type=F32))
        o_ref[a * ts:(a + 1) * ts, :] = (o / l).astype(o_ref.dtype)


def na_call(proj, tok0, b, t, ck, cv, bias):
    rows = t // GRID_W
    n_rb = rows // NA_QROWS
    qtok = NA_QROWS * GRID_W
    ctok = NA_CHUNK_ROWS * GRID_W
    lc = ck.shape[1]
    q0 = tok0 // qtok
    c0 = tok0 // ctok
    cps = t // ctok

    def wchunk(rb):
        return jnp.clip(rb * (NA_QROWS // NA_CHUNK_ROWS) - 1, 0, (rows - NA_WROWS) // NA_CHUNK_ROWS)

    def geo(rb):
        return jnp.where(rb == 0, 0, jnp.where(rb == n_rb - 1, 2, 1))

    def kspec(c, colblk):
        return pl.BlockSpec((ctok, LANES),
                            lambda h, rb, bi: (c0 + bi * cps + wchunk(rb) + c, colblk + h))

    in_specs = ([pl.BlockSpec((qtok, LANES), lambda h, rb, bi: (q0 + bi * n_rb + rb, h))]
                + [kspec(c, C_HEADS) for c in range(NA_NCHUNK)]
                + [kspec(c, 2 * C_HEADS) for c in range(NA_NCHUNK)]
                + [pl.BlockSpec((None, lc, LANES), lambda h, rb, bi: (bi, 0, h)),
                   pl.BlockSpec((None, lc, LANES), lambda h, rb, bi: (bi, 0, h)),
                   pl.BlockSpec((None, None, qtok, NA_WROWS * GRID_W),
                                lambda h, rb, bi: (h, geo(rb), 0, 0))])
    return pl.pallas_call(
        _na_kernel,
        grid=(C_HEADS, n_rb, b),
        in_specs=in_specs,
        out_specs=pl.BlockSpec((qtok, LANES), lambda h, rb, bi: (bi * n_rb + rb, h)),
        out_shape=jax.ShapeDtypeStruct((b * t, C_WIDTH), BF16),
        compiler_params=_cparams(("arbitrary", "arbitrary", "arbitrary")),
        name="neighbourhood_attention",
    )(*([proj] * (1 + 2 * NA_NCHUNK)), ck, cv, bias)


LRU_HALO = SUBLANES


def _expm1(x):
    u = jnp.exp(x)
    return jnp.where(u == 1.0, x, (u - 1.0) * x / jnp.log(u))


def _lru_gates_kernel(x_ref, prev_ref, next_ref, cw_ref, cb_ref, wa_ref, ba_ref, wi_ref, bi_ref,
                      lam_ref, af_ref, bf_ref, ab_ref, bb_ref, *, tt, n_prompt, t_p, t_s):
    i = pl.program_id(0)
    s = i * tt
    in_prompt = s < n_prompt
    pos = jnp.where(in_prompt, s % t_p, (s - n_prompt) % t_s)
    seq_len = jnp.where(in_prompt, t_p, t_s)
    is_start = pos == 0
    is_end = pos + tt == seq_len
    x = x_ref[...]
    prev = jnp.where(is_start, 0.0, prev_ref[LRU_HALO - 2:LRU_HALO, :])
    nxt = jnp.where(is_end, 0.0, next_ref[0:1, :])
    xp = jnp.concatenate([prev, x, nxt], axis=0)
    cw = cw_ref[...]
    xc = cb_ref[...] + xp[0:tt] * cw[0:1]
    for j in range(1, CONV_W):
        xc = xc + xp[j:j + tt] * cw[j:j + 1]
    outs = ((af_ref, bf_ref), (ab_ref, bb_ref))
    for n in range(LRU_BLOCKS):
        sl = slice(n * LRU_BLOCK, (n + 1) * LRU_BLOCK)
        xb = xc[:, sl]
        xb16 = xb.astype(BF16)
        for dr in range(2):
            r = _sigmoid(jnp.dot(xb16, wa_ref[dr, n], preferred_element_type=F32) + ba_ref[dr:dr + 1, sl])
            g = _sigmoid(jnp.dot(xb16, wi_ref[dr, n], preferred_element_type=F32) + bi_ref[dr:dr + 1, sl])
            z = -lam_ref[dr:dr + 1, sl]
            softplus = jnp.maximum(z, 0.0) + jnp.log(1.0 + jnp.exp(-jnp.abs(z)))
            log_a = -LRU_C * r * softplus
            a_out, b_out = outs[dr]
            a_out[:, sl] = jnp.exp(log_a)
            b_out[:, sl] = jnp.sqrt(-_expm1(2.0 * log_a)) * g * xb


def lru_gates(proj, conv_w, conv_b, w_a, b_a, w_i, b_i, lam, *, n_prompt, t_p, t_s, tt=256):
    n = proj.shape[0]
    xcol = (3 * A_WIDTH) // LRU_WIDTH
    hb = tt // LRU_HALO
    nh = n // LRU_HALO
    full = lambda shape: pl.BlockSpec(shape, lambda i: (0,) * len(shape))
    out = jax.ShapeDtypeStruct((n, LRU_WIDTH), F32)
    ospec = pl.BlockSpec((tt, LRU_WIDTH), lambda i: (i, 0))
    return pl.pallas_call(
        functools.partial(_lru_gates_kernel, tt=tt, n_prompt=n_prompt, t_p=t_p, t_s=t_s),
        grid=(n // tt,),
        in_specs=[pl.BlockSpec((tt, LRU_WIDTH), lambda i: (i, xcol)),
                  pl.BlockSpec((LRU_HALO, LRU_WIDTH), lambda i: (jnp.maximum(i * hb - 1, 0), xcol)),
                  pl.BlockSpec((LRU_HALO, LRU_WIDTH), lambda i: (jnp.minimum((i + 1) * hb, nh - 1), xcol)),
                  full((CONV_W, LRU_WIDTH)), full((1, LRU_WIDTH)),
                  full((2, LRU_BLOCKS, LRU_BLOCK, LRU_BLOCK)), full((2, LRU_WIDTH)),
                  full((2, LRU_BLOCKS, LRU_BLOCK, LRU_BLOCK)), full((2, LRU_WIDTH)),
                  full((2, LRU_WIDTH))],
        out_specs=[ospec] * 4,
        out_shape=[out] * 4,
        compiler_params=_cparams(("arbitrary",)),
        name="lru_gates",
    )(proj, proj, proj, conv_w, conv_b.reshape(1, LRU_WIDTH), w_a.astype(BF16), b_a,
      w_i.astype(BF16), b_i, lam)


def _lru_scan_kernel(*refs, tt, reverse, combine):
    if combine:
        a_ref, b_ref, h0_ref, hf_ref, g_ref, o_ref, fin_ref, h_scr = refs
    else:
        a_ref, b_ref, h0_ref, o_ref, fin_ref, h_scr = refs
    ti = pl.program_id(1)

    @pl.when(ti == 0)
    def _():
        h_scr[...] = h0_ref[...]

    def step(j, h):
        t = tt - 1 - j if reverse else j
        h = a_ref[t] * h + b_ref[t]
        if combine:
            o_ref[t] = ((hf_ref[t] + h) * _gelu(g_ref[t])).astype(o_ref.dtype)
        else:
            o_ref[t] = h
        return h

    h = lax.fori_loop(0, tt, step, h_scr[...], unroll=8)
    h_scr[...] = h
    fin_ref[...] = h


def lru_scan(a, b, h0, tok0, nb, t, *, reverse, hf=None, proj=None, tt=256):
    tt = min(tt, t)
    nt = t // tt
    r0 = tok0 // tt
    combine = hf is not None
    vreg = (SUBLANES, LANES)
    view = lambda z: z.reshape(z.shape[0], LRU_WIDTH // LANES, LANES)

    def tmap(bi, ti):
        return nt - 1 - ti if reverse else ti

    in_specs = [pl.BlockSpec((tt,) + vreg, lambda bi, ti: (r0 + bi * nt + tmap(bi, ti), 0, 0)),
                pl.BlockSpec((tt,) + vreg, lambda bi, ti: (r0 + bi * nt + tmap(bi, ti), 0, 0)),
                pl.BlockSpec((None,) + vreg, lambda bi, ti: (bi, 0, 0))]
    args = [view(a), view(b), view(h0)]
    if combine:
        gcol = (3 * A_WIDTH + LRU_WIDTH) // LRU_WIDTH
        in_specs += [pl.BlockSpec((tt,) + vreg, lambda bi, ti: (bi * nt + tmap(bi, ti), 0, 0)),
                     pl.BlockSpec((tt,) + vreg, lambda bi, ti: (r0 + bi * nt + tmap(bi, ti), gcol, 0))]
        args += [view(hf), proj.reshape(proj.shape[0], EVEN_IN // LANES, LANES)]
    out, fin = pl.pallas_call(
        functools.partial(_lru_scan_kernel, tt=tt, reverse=reverse, combine=combine),
        grid=(nb, nt),
        in_specs=in_specs,
        out_specs=[pl.BlockSpec((tt,) + vreg, lambda bi, ti: (bi * nt + tmap(bi, ti), 0, 0)),
                   pl.BlockSpec((None,) + vreg, lambda bi, ti: (bi, 0, 0))],
        out_shape=[jax.ShapeDtypeStruct((nb * t,) + vreg, F32),
                   jax.ShapeDtypeStruct((nb,) + vreg, F32)],
        scratch_shapes=[pltpu.VMEM(vreg, F32)],
        compiler_params=_cparams(("arbitrary", "arbitrary")),
        name="lru_scan_bwd" if reverse else "lru_scan_fwd",
    )(*args)
    return out.reshape(nb * t, LRU_WIDTH), fin.reshape(nb, LRU_WIDTH)


def _topk_rows(s_list, k, payloads=None):
    n = s_list[0].shape[0]
    iota = lax.broadcasted_iota(jnp.int32, s_list[0].shape, 0).astype(F32)
    s_list = list(s_list)
    vals = [[] for _ in s_list]
    idxs = [[] for _ in s_list]
    for _ in range(k):
        for j, s in enumerate(s_list):
            m = jnp.max(s, axis=0, keepdims=True)
            am = jnp.min(jnp.where(s == m, iota, float(n)), axis=0, keepdims=True)
            sel = iota == am
            vals[j].append(m)
            if payloads is None:
                idxs[j].append(am)
            else:
                idxs[j].append(jnp.sum(jnp.where(sel, payloads[j], 0.0), axis=0, keepdims=True))
            s_list[j] = jnp.where(sel, -jnp.inf, s)
    return [(jnp.concatenate(v, axis=0), jnp.concatenate(i, axis=0)) for v, i in zip(vals, idxs)]


def _topk_rows_paired(s_list, k):
    n = s_list[0].shape[0]
    half = n // 2
    iota = lax.broadcasted_iota(jnp.int32, (half,) + s_list[0].shape[1:], 0).astype(F32)
    iota_b = iota + float(half)
    state = []
    for s in s_list:
        a, b = s[:half], s[half:]
        ge = a >= b
        state.append((jnp.where(ge, a, b), jnp.where(ge, b, a),
                      jnp.where(ge, iota, iota_b), jnp.where(ge, iota_b, iota)))
    vals = [[] for _ in s_list]
    idxs = [[] for _ in s_list]
    for _ in range(k):
        for j, (hi, lo, ih, il) in enumerate(state):
            m = jnp.max(hi, axis=0, keepdims=True)
            am = jnp.min(jnp.where(hi == m, ih, float(n)), axis=0, keepdims=True)
            sel = ih == am
            vals[j].append(m)
            idxs[j].append(am)
            state[j] = (jnp.where(sel, lo, hi), jnp.where(sel, -jnp.inf, lo),
                        jnp.where(sel, il, ih), il)
    return [(jnp.concatenate(v, axis=0), jnp.concatenate(i, axis=0)) for v, i in zip(vals, idxs)]


def _peer_route_kernel(q_ref, sk_ref, idx_ref, gate_ref, *, tm):
    half = PEER_QDIM // 2
    n_chunk = tm // LANES
    scores = []
    for c in range(n_chunk):
        q = q_ref[c * LANES:(c + 1) * LANES, :].astype(BF16)
        for p in range(2):
            scores.append(lax.dot_general(sk_ref[p], q[:, p * half:(p + 1) * half], NT_DIMS,
                                          preferred_element_type=F32))
    tops = _topk_rows_paired(scores, PEER_TOPK)
    nb = [PEER_TOPK // (a + 1) for a in range(PEER_TOPK)]
    pad = -sum(nb) % SUBLANES
    cands, cidxs = [], []
    for c in range(n_chunk):
        (s1, i1), (s2, i2) = tops[2 * c], tops[2 * c + 1]
        cands.append(jnp.concatenate([s1[a:a + 1] + s2[:nb[a]] for a in range(PEER_TOPK)]
                                     + [jnp.full((pad, LANES), -jnp.inf, F32)], axis=0))
        cidxs.append(jnp.concatenate(
            [i1[a:a + 1] * float(PEER_NKEYS) + i2[:nb[a]] for a in range(PEER_TOPK)]
            + [jnp.zeros((pad, LANES), F32)], axis=0))
    best = _topk_rows(cands, PEER_TOPK, payloads=cidxs)
    for c, (bs, bidx) in enumerate(best):
        e = jnp.exp(bs - bs[0:1])
        gates = e / jnp.sum(e, axis=0, keepdims=True)
        idx_ref[:, c * LANES:(c + 1) * LANES] = bidx.astype(jnp.int32)
        gate_ref[:, c * LANES:(c + 1) * LANES] = gates


def peer_route(q, sub_keys, tm=512):
    n = q.shape[0]
    half = PEER_QDIM // 2
    return pl.pallas_call(
        functools.partial(_peer_route_kernel, tm=tm),
        grid=(n // tm, PEER_HEADS),
        in_specs=[pl.BlockSpec((tm, PEER_QDIM), lambda i, h: (i, h)),
                  pl.BlockSpec((None, 2, PEER_NKEYS, half), lambda i, h: (h, 0, 0, 0))],
        out_specs=[pl.BlockSpec((None, PEER_TOPK, tm), lambda i, h: (h, 0, i))] * 2,
        out_shape=[jax.ShapeDtypeStruct((PEER_HEADS, PEER_TOPK, n), jnp.int32),
                   jax.ShapeDtypeStruct((PEER_HEADS, PEER_TOPK, n), F32)],
        compiler_params=_cparams(("arbitrary", "arbitrary")),
        name="peer_route",
    )(q, sub_keys)


def _peer_gates_kernel(idx_ref, gate_ref, g_ref):
    idx = idx_ref[...]
    gate = gate_ref[...]
    tg, _, hk = idx.shape
    i1 = (idx // PEER_NKEYS).astype(F32).astype(BF16)
    i2 = (idx % PEER_NKEYS).astype(F32).astype(BF16)
    iota = lax.broadcasted_iota(jnp.int32, (1, PEER_NKEYS, hk), 1).astype(F32).astype(BF16)
    zero = jnp.zeros((), BF16)
    a = jnp.where(i1 == iota, gate.astype(BF16), zero)
    bsel = jnp.where(i2 == iota, jnp.ones((), BF16), zero)
    g3 = lax.dot_general(a, bsel, (((2,), (2,)), ((0,), (0,))), preferred_element_type=F32)
    g_ref[...] = pltpu.einshape("nij->inj", g3).astype(g_ref.dtype)


def peer_gate_rows(idx, gates, tg=64):
    n, hk = idx.shape
    return pl.pallas_call(
        _peer_gates_kernel,
        grid=(n // tg,),
        in_specs=[pl.BlockSpec((tg, 1, hk), lambda i: (i, 0, 0))] * 2,
        out_specs=pl.BlockSpec((PEER_NKEYS, tg, PEER_NKEYS), lambda i: (0, i, 0)),
        out_shape=jax.ShapeDtypeStruct((PEER_NKEYS, n, PEER_NKEYS), BF16),
        compiler_params=_cparams(("arbitrary",)),
        name="peer_gate_rows",
    )(idx.reshape(n, 1, hk), gates.reshape(n, 1, hk))


def _peer_dense_kernel(h_ref, ut_ref, v_ref, gr_ref, x_ref, mg_ref, fg_ref, o_ref, *, final_norm,
                       row_split):
    e = pl.program_id(1)

    @pl.when(e == 0)
    def _():
        o_ref[...] = jnp.zeros_like(o_ref)

    tr = h_ref.shape[0] // row_split
    scores = [jnp.dot(h_ref[r * tr:(r + 1) * tr, :], ut_ref[...], preferred_element_type=F32)
              for r in range(row_split)]
    for r, s in enumerate(scores):
        rows = slice(r * tr, (r + 1) * tr)
        act = jnp.concatenate(
            [(_gelu(s[:, j * PEER_NKEYS:(j + 1) * PEER_NKEYS])
              * gr_ref[j, rows, :].astype(F32)).astype(BF16)
             for j in range(gr_ref.shape[0])], axis=1)
        o_ref[rows, :] += jnp.dot(act, v_ref[...], preferred_element_type=F32)

    @pl.when(e == pl.num_programs(1) - 1)
    def _():
        y = x_ref[...] + mg_ref[...] * o_ref[...]
        if final_norm:
            y = y * lax.rsqrt(jnp.mean(y * y, axis=-1, keepdims=True) + EPS) * fg_ref[...]
        o_ref[...] = y


def peer_dense(h, ut16, v16, grows, x, mod, gate_idx, final_g, *, n_prompt, t_s, final_norm,
               tm=512, te=1024, row_split=2):
    n, d = x.shape
    ne = v16.shape[0]
    row = functools.partial(_mod_row, tm=tm, n_prompt=n_prompt, t_s=t_s)
    return pl.pallas_call(
        functools.partial(_peer_dense_kernel, final_norm=final_norm, row_split=row_split),
        grid=(n // tm, ne // te),
        in_specs=[pl.BlockSpec((tm, d), lambda i, e: (i, 0)),
                  pl.BlockSpec((d, te), lambda i, e: (0, e)),
                  pl.BlockSpec((te, d), lambda i, e: (e, 0)),
                  pl.BlockSpec((te // PEER_NKEYS, tm, PEER_NKEYS), lambda i, e: (e, i, 0)),
                  pl.BlockSpec((tm, d), lambda i, e: (i, 0)),
                  pl.BlockSpec((None, None, 1, d), lambda i, e: (row(i), gate_idx, 0, 0)),
                  pl.BlockSpec((1, d), lambda i, e: (0, 0))],
        out_specs=pl.BlockSpec((tm, d), lambda i, e: (i, 0)),
        out_shape=jax.ShapeDtypeStruct((n, d), F32),
        compiler_params=_cparams(("arbitrary", "arbitrary")),
        name="peer_dense",
    )(h, ut16, v16, grows, x, mod, final_g.reshape(1, d))


def peer_block(x, norm_g, mod, w_q16, sk16, ut16, v16, final_g, *, n_prompt, t_s, final_norm):
    n = x.shape[0]
    q, h = norm_matmul(x, norm_g, mod, 3, 4, w_q16, n_prompt=n_prompt, t_s=t_s, emit_h=True)
    idx, gates = peer_route(q, sk16)
    hk = PEER_HEADS * PEER_TOPK
    idx = idx.reshape(hk, n).T
    gates = gates.reshape(hk, n).T
    grows = peer_gate_rows(idx, gates)
    return peer_dense(h, ut16, v16, grows, x, mod, 5, final_g, n_prompt=n_prompt, t_s=t_s,
                      final_norm=final_norm)


def kernel(x_prompt, x_sample, cache_a_k, cache_a_v, state_lru, cache_c_k, cache_c_v, c, c_ctx, w_mod, b_mod, norm1_g, norm2_g, final_norm_g, even_w_in, even_w_out, a_lambda, a_subln_g, lru_conv_w, lru_conv_b, lru_w_a, lru_b_a, lru_w_i, lru_b_i, lru_lambda, odd_w_in, odd_w_out, na_rpb, peer_w_q, peer_sub_keys, peer_u, peer_v):
    b_p, t_p, d = x_prompt.shape
    b_s, t_s, _ = x_sample.shape
    depth = w_mod.shape[0]
    n_p = b_p * t_p
    n_s = b_s * t_s
    lc = cache_a_k.shape[2]
    tok = dict(n_prompt=n_p, t_s=t_s)

    x = jnp.concatenate([x_prompt.reshape(n_p, d), x_sample.reshape(n_s, d)], axis=0)
    n_rows = -(-(1 + b_s) // SUBLANES) * SUBLANES
    cond = jnp.concatenate([c_ctx[None], c, jnp.zeros((n_rows - 1 - b_s, d), F32)], axis=0)
    mod_all = modulation_all(cond, w_mod, b_mod).reshape(depth, n_rows, 6, 1, d)
    rope = rope_tables(t_s)

    new_ak, new_av, new_lru, new_ck, new_cv = [], [], [], [], []
    for l in range(depth):
        mod = mod_all[l]
        if l % 2 == 0:
            e = l // 2
            lam_init = 0.8 - 0.6 * math.exp(-0.3 * l)
            proj = norm_matmul(x, norm1_g[l], mod, 0, 1, even_w_in[e].astype(BF16), **tok)
            ctx = (cache_a_k[:, e].reshape(b_s, lc, A_WIDTH), cache_a_v[:, e].reshape(b_s, lc, A_WIDTH))
            oa_p = diff_attention_call(proj, 0, b_p, t_p, a_lambda[e], a_subln_g[e], lam_init)
            oa_s = diff_attention_call(proj, n_p, b_s, t_s, a_lambda[e], a_subln_g[e], lam_init,
                                       ctx=ctx, rope=rope)
            a_f, b_f, a_b, b_b = lru_gates(proj, lru_conv_w[e], lru_conv_b[e], lru_w_a[e], lru_b_a[e],
                                           lru_w_i[e], lru_b_i[e], lru_lambda[e],
                                           n_prompt=n_p, t_p=t_p, t_s=t_s)
            zeros_p = jnp.zeros((b_p, LRU_WIDTH), F32)
            hf_p, fin_f = lru_scan(a_f, b_f, zeros_p, 0, b_p, t_p, reverse=False)
            ob_p, fin_b = lru_scan(a_b, b_b, zeros_p, 0, b_p, t_p, reverse=True, hf=hf_p, proj=proj)
            hf_s, _ = lru_scan(a_f, b_f, state_lru[:, e, 0], n_p, b_s, t_s, reverse=False)
            ob_s, _ = lru_scan(a_b, b_b, state_lru[:, e, 1], n_p, b_s, t_s, reverse=True,
                               hf=hf_s, proj=proj)
            w_out = even_w_out[e].astype(BF16)
            x = matmul_res([(oa_p, oa_s), (ob_p, ob_s)], [w_out[:A_WIDTH], w_out[A_WIDTH:]],
                           x, mod, 2, **tok)
            new_ak.append(proj[:n_p, A_WIDTH:2 * A_WIDTH].reshape(b_p, t_p, A_HEADS, 2 * A_QK_DIM))
            new_av.append(proj[:n_p, 2 * A_WIDTH:3 * A_WIDTH].reshape(b_p, t_p, A_HEADS, A_V_DIM))
            new_lru.append(jnp.stack([fin_f, fin_b], axis=1))
        else:
            o = l // 2
            proj = norm_matmul(x, norm1_g[l], mod, 0, 1, odd_w_in[o].astype(BF16), **tok)
            oc_p = attention_call(proj, b_p, t_p)
            bias = na_bias_tables(na_rpb[o], t_s // GRID_W)
            oc_s = na_call(proj, n_p, b_s, t_s, cache_c_k[:, o].reshape(b_s, lc, C_WIDTH),
                           cache_c_v[:, o].reshape(b_s, lc, C_WIDTH), bias)
            x = matmul_res([(oc_p, oc_s)], [odd_w_out[o].astype(BF16)], x, mod, 2, **tok)
            new_ck.append(proj[:n_p, C_WIDTH:2 * C_WIDTH].reshape(b_p, t_p, C_HEADS, C_HEAD_DIM))
            new_cv.append(proj[:n_p, 2 * C_WIDTH:3 * C_WIDTH].reshape(b_p, t_p, C_HEADS, C_HEAD_DIM))
        x = peer_block(x, norm2_g[l], mod, peer_w_q[l].astype(BF16), peer_sub_keys[l].astype(BF16),
                       peer_u[l].astype(BF16).T, peer_v[l].astype(BF16), final_norm_g,
                       final_norm=(l == depth - 1), **tok)
    y_prompt = x[:n_p].reshape(b_p, t_p, d)
    y_sample = x[n_p:].reshape(b_s, t_s, d)
    return (y_prompt, y_sample, jnp.stack(new_ak, axis=1), jnp.stack(new_av, axis=1),
            jnp.stack(new_lru, axis=1), jnp.stack(new_ck, axis=1), jnp.stack(new_cv, axis=1))
```

```python
import functools
import math

import numpy as np
import jax
import jax.numpy as jnp
from jax import lax
from jax.experimental import pallas as pl
from jax.experimental.pallas import tpu as pltpu

F32 = jnp.float32
BF16 = jnp.bfloat16

D_MODEL = 2048
GRID_W = 64
EPS = 1e-6
ROPE_BASE = 10000.0
A_HEADS = 8
A_QK_DIM = 64
A_V_DIM = 128
A_WIDTH = A_HEADS * A_V_DIM
LRU_WIDTH = 1024
LRU_BLOCKS = 8
LRU_BLOCK = LRU_WIDTH // LRU_BLOCKS
CONV_W = 4
LRU_C = 8.0
EVEN_IN = 3 * A_WIDTH + 2 * LRU_WIDTH
C_HEADS = 16
C_HEAD_DIM = 128
C_WIDTH = C_HEADS * C_HEAD_DIM
NA_KH = 8
NA_KW = 16
PEER_HEADS = 8
PEER_NKEYS = 128
PEER_N = PEER_NKEYS * PEER_NKEYS
PEER_QDIM = 256
PEER_TOPK = 16

LANES = 128
SUBLANES = 8
VMEM_LIMIT = 60 * 1024 * 1024

NT_DIMS = (((1,), (1,)), ((), ()))


def _cparams(sem):
    return pltpu.CompilerParams(dimension_semantics=sem, vmem_limit_bytes=VMEM_LIMIT)


def _gelu(x):
    c = math.sqrt(2.0 / math.pi)
    return 0.5 * x * (1.0 + jnp.tanh(c * (x + 0.044715 * (x * x * x))))


def _sigmoid(x):
    return 1.0 / (1.0 + jnp.exp(-x))


def _mod_kernel(c_ref, w_ref, b_ref, o_ref):
    c = c_ref[...]
    s = c * _sigmoid(c)
    o_ref[0] = jnp.dot(s.astype(BF16), w_ref[0].astype(BF16),
                       preferred_element_type=F32) + b_ref[0]


def modulation_all(cond, w_mod, b_mod):
    depth, d, n6 = w_mod.shape
    r = cond.shape[0]
    tn = 768
    return pl.pallas_call(
        _mod_kernel,
        grid=(depth, n6 // tn),
        in_specs=[pl.BlockSpec((r, d), lambda l, j: (0, 0)),
                  pl.BlockSpec((1, d, tn), lambda l, j: (l, 0, j)),
                  pl.BlockSpec((1, 1, tn), lambda l, j: (l, 0, j))],
        out_specs=pl.BlockSpec((1, r, tn), lambda l, j: (l, 0, j)),
        out_shape=jax.ShapeDtypeStruct((depth, r, n6), F32),
        compiler_params=_cparams(("arbitrary", "arbitrary")),
        name="modulation",
    )(cond, w_mod, b_mod.reshape(depth, 1, n6))


def _mod_row(i, tm, n_prompt, t_s):
    s = i * tm
    return jnp.where(s < n_prompt, 0, 1 + (s - n_prompt) // t_s)


def _norm_matmul_kernel(x_ref, g_ref, sh_ref, sc_ref, w_ref, o_ref, *rest, emit_h):
    if emit_h:
        h_out_ref, h_scr = rest
    else:
        (h_scr,) = rest

    @pl.when(pl.program_id(1) == 0)
    def _():
        x = x_ref[...]
        y = x * lax.rsqrt(jnp.mean(x * x, axis=-1, keepdims=True) + EPS)
        h = (y * g_ref[...]) * (1.0 + sc_ref[...]) + sh_ref[...]
        h_scr[...] = h.astype(BF16)
        if emit_h:
            h_out_ref[...] = h.astype(BF16)

    o_ref[...] = jnp.dot(h_scr[...], w_ref[...], preferred_element_type=F32).astype(o_ref.dtype)


def norm_matmul(x, g, mod, shift_idx, scale_idx, w, *, n_prompt, t_s, emit_h=False,
                tm=1024, tn=1024, out_dtype=F32):
    n, d = x.shape
    nout = w.shape[1]
    row = functools.partial(_mod_row, tm=tm, n_prompt=n_prompt, t_s=t_s)
    out_shape = [jax.ShapeDtypeStruct((n, nout), out_dtype)]
    out_specs = [pl.BlockSpec((tm, tn), lambda i, j: (i, j))]
    if emit_h:
        out_shape.append(jax.ShapeDtypeStruct((n, d), BF16))
        out_specs.append(pl.BlockSpec((tm, d), lambda i, j: (i, 0)))
    res = pl.pallas_call(
        functools.partial(_norm_matmul_kernel, emit_h=emit_h),
        grid=(n // tm, nout // tn),
        in_specs=[pl.BlockSpec((tm, d), lambda i, j: (i, 0)),
                  pl.BlockSpec((1, d), lambda i, j: (0, 0)),
                  pl.BlockSpec((None, None, 1, d), lambda i, j: (row(i), shift_idx, 0, 0)),
                  pl.BlockSpec((None, None, 1, d), lambda i, j: (row(i), scale_idx, 0, 0)),
                  pl.BlockSpec((d, tn), lambda i, j: (0, j))],
        out_specs=out_specs,
        out_shape=out_shape,
        scratch_shapes=[pltpu.VMEM((tm, d), BF16)],
        compiler_params=_cparams(("arbitrary", "arbitrary")),
        name="norm_matmul",
    )(x, g.reshape(1, d), mod, mod, w)
    return res if emit_h else res[0]


def _matmul_res_kernel(*refs, n_in, np_tiles):
    ap_refs = refs[:n_in]
    as_refs = refs[n_in:2 * n_in]
    w_refs = refs[2 * n_in:3 * n_in]
    x_ref, gate_ref, o_ref = refs[3 * n_in:]

    def emit(a_refs):
        acc = None
        for a_ref, w_ref in zip(a_refs, w_refs):
            p = jnp.dot(a_ref[...].astype(BF16), w_ref[...], preferred_element_type=F32)
            acc = p if acc is None else acc + p
        o_ref[...] = x_ref[...] + gate_ref[...] * acc

    i = pl.program_id(0)
    pl.when(i < np_tiles)(lambda: emit(ap_refs))
    pl.when(i >= np_tiles)(lambda: emit(as_refs))


def matmul_res(a_pairs, w_list, x, mod, gate_idx, *, n_prompt, t_s, tm=1024, tn=512):
    n, d = x.shape
    n_in = len(a_pairs)
    np_tiles = n_prompt // tm
    ns_tiles = n // tm - np_tiles
    row = functools.partial(_mod_row, tm=tm, n_prompt=n_prompt, t_s=t_s)
    in_specs = ([pl.BlockSpec((tm, ap.shape[1]), lambda i, j: (jnp.minimum(i, np_tiles - 1), 0))
                 for ap, _ in a_pairs]
                + [pl.BlockSpec((tm, a_s.shape[1]),
                                lambda i, j: (jnp.clip(i - np_tiles, 0, ns_tiles - 1), 0))
                   for _, a_s in a_pairs]
                + [pl.BlockSpec((w.shape[0], tn), lambda i, j: (0, j)) for w in w_list]
                + [pl.BlockSpec((tm, tn), lambda i, j: (i, j)),
                   pl.BlockSpec((None, None, 1, tn), lambda i, j: (row(i), gate_idx, 0, j))])
    return pl.pallas_call(
        functools.partial(_matmul_res_kernel, n_in=n_in, np_tiles=np_tiles),
        grid=(n // tm, d // tn),
        in_specs=in_specs,
        out_specs=pl.BlockSpec((tm, tn), lambda i, j: (i, j)),
        out_shape=jax.ShapeDtypeStruct((n, d), F32),
        compiler_params=_cparams(("arbitrary", "arbitrary")),
        name="matmul_res",
    )(*[ap for ap, _ in a_pairs], *[a_s for _, a_s in a_pairs], *w_list, x, mod)


def _rope(x, cos, sin_signed):
    lane = lax.broadcasted_iota(jnp.int32, x.shape, 1)
    first = (lane % 32) < 16
    partner = jnp.where(first, pltpu.roll(x, LANES - 16, axis=1), pltpu.roll(x, 16, axis=1))
    return x * cos + partner * sin_signed


def _diff_attn_kernel(*refs, has_ctx, lam_init, tq, n_sub):
    if has_ctx:
        (q_ref, k_ref, v_ref, ck_ref, cv_ref, cos_ref, sin_ref, lam_ref, g_ref,
         o_ref, k_scr, v_scr, ck_scr, cv_scr) = refs
    else:
        q_ref, k_ref, v_ref, lam_ref, g_ref, o_ref, k_scr, v_scr = refs
    qi = pl.program_id(2)

    @pl.when(qi == 0)
    def _():
        k = k_ref[...]
        if has_ctx:
            k = _rope(k, cos_ref[...], sin_ref[...])
            ck_scr[...] = ck_ref[...].astype(BF16)
            cv_scr[...] = cv_ref[...].astype(BF16)
        k_scr[...] = k.astype(BF16)
        v_scr[...] = v_ref[...].astype(BF16)

    q = q_ref[...]
    if has_ctx:
        off = pl.multiple_of(qi * tq, tq)
        q = _rope(q, cos_ref[pl.ds(off, tq), :], sin_ref[pl.ds(off, tq), :])
    q = q * (A_QK_DIM ** -0.5)
    lp = lam_ref[...]
    lam = (jnp.exp(jnp.sum(lp[0:1] * lp[1:2], axis=-1, keepdims=True))
           - jnp.exp(jnp.sum(lp[2:3] * lp[3:4], axis=-1, keepdims=True)) + lam_init)
    ts = tq // n_sub
    lane = lax.broadcasted_iota(jnp.int32, (ts, LANES), 1)

    scored = []
    for a in range(n_sub):
        qa = q[a * ts:(a + 1) * ts]
        q2 = jnp.concatenate([jnp.where(lane < A_QK_DIM, qa, 0.0),
                              jnp.where(lane >= A_QK_DIM, qa, 0.0)], axis=0).astype(BF16)
        s = lax.dot_general(q2, k_scr[...], NT_DIMS, preferred_element_type=F32)
        mx = jnp.max(s, axis=-1, keepdims=True)
        sc = None
        if has_ctx:
            sc = lax.dot_general(q2, ck_scr[...], NT_DIMS, preferred_element_type=F32)
            mx = jnp.maximum(mx, jnp.max(sc, axis=-1, keepdims=True))
        scored.append((s, sc, mx))
    for a, (s, sc, mx) in enumerate(scored):
        e = jnp.exp(s - mx)
        l = jnp.sum(e, axis=-1, keepdims=True)
        if has_ctx:
            ec = jnp.exp(sc - mx)
            l = l + jnp.sum(ec, axis=-1, keepdims=True)
        r = 1.0 / l
        r0 = r[:ts]
        r1 = lam * r[ts:]
        d = jnp.dot((e[:ts] * r0 - e[ts:] * r1).astype(BF16), v_scr[...],
                    preferred_element_type=F32)
        if has_ctx:
            d = d + jnp.dot((ec[:ts] * r0 - ec[ts:] * r1).astype(BF16), cv_scr[...],
                            preferred_element_type=F32)
        y = d * lax.rsqrt(jnp.mean(d * d, axis=-1, keepdims=True) + EPS)
        o_ref[a * ts:(a + 1) * ts, :] = ((y * g_ref[...]) * (1.0 - lam_init)).astype(o_ref.dtype)


def diff_attention_call(proj, tok0, b, t, lam_p, subln_g, lam_init, ctx=None, rope=None, tq=512,
                        n_sub=4):
    tq = min(tq, t)
    nq = t // tq
    q0 = tok0 // tq
    k0 = tok0 // t
    has_ctx = ctx is not None
    in_specs = [pl.BlockSpec((tq, LANES), lambda bi, h, qi: (q0 + bi * nq + qi, h)),
                pl.BlockSpec((t, LANES), lambda bi, h, qi: (k0 + bi, A_HEADS + h)),
                pl.BlockSpec((t, LANES), lambda bi, h, qi: (k0 + bi, 2 * A_HEADS + h))]
    args = [proj, proj, proj]
    if has_ctx:
        ck, cv = ctx
        lc = ck.shape[1]
        cos, sin = rope
        in_specs += [pl.BlockSpec((None, lc, LANES), lambda bi, h, qi: (bi, 0, h)),
                     pl.BlockSpec((None, lc, LANES), lambda bi, h, qi: (bi, 0, h)),
                     pl.BlockSpec((t, LANES), lambda bi, h, qi: (0, 0)),
                     pl.BlockSpec((t, LANES), lambda bi, h, qi: (0, 0))]
        args += [ck, cv, cos, sin]
    in_specs += [pl.BlockSpec((4, A_QK_DIM), lambda bi, h, qi: (0, 0)),
                 pl.BlockSpec((1, A_V_DIM), lambda bi, h, qi: (0, 0))]
    args += [lam_p, subln_g.reshape(1, A_V_DIM)]
    scratch = [pltpu.VMEM((t, LANES), BF16), pltpu.VMEM((t, LANES), BF16)]
    if has_ctx:
        scratch += [pltpu.VMEM((lc, LANES), BF16), pltpu.VMEM((lc, LANES), BF16)]
    return pl.pallas_call(
        functools.partial(_diff_attn_kernel, has_ctx=has_ctx, lam_init=lam_init, tq=tq, n_sub=n_sub),
        grid=(b, A_HEADS, nq),
        in_specs=in_specs,
        out_specs=pl.BlockSpec((tq, LANES), lambda bi, h, qi: (bi * nq + qi, h)),
        out_shape=jax.ShapeDtypeStruct((b * t, A_WIDTH), BF16),
        scratch_shapes=scratch,
        compiler_params=_cparams(("arbitrary", "arbitrary", "arbitrary")),
        name="diff_attention",
    )(*args)


def rope_tables(t):
    tok = jnp.arange(t)
    row = (tok // GRID_W).astype(F32)
    col = (tok % GRID_W).astype(F32)
    n_freq = A_QK_DIM // 4
    inv = ROPE_BASE ** (-jnp.arange(n_freq, dtype=F32) / n_freq)
    ang_r = row[:, None] * inv
    ang_c = col[:, None] * inv
    cos64 = jnp.concatenate([jnp.cos(ang_r), jnp.cos(ang_r), jnp.cos(ang_c), jnp.cos(ang_c)], axis=1)
    sin64 = jnp.concatenate([-jnp.sin(ang_r), jnp.sin(ang_r), -jnp.sin(ang_c), jnp.sin(ang_c)], axis=1)
    return jnp.tile(cos64, (1, 2)), jnp.tile(sin64, (1, 2))


def _attn_kernel(q_ref, k_ref, v_ref, o_ref):
    q = q_ref[...].astype(BF16)
    k = k_ref[...].astype(BF16)
    s = lax.dot_general(q, k, NT_DIMS, preferred_element_type=F32) * (C_HEAD_DIM ** -0.5)
    mx = jnp.max(s, axis=-1, keepdims=True)
    e = jnp.exp(s - mx)
    l = jnp.sum(e, axis=-1, keepdims=True)
    o = jnp.dot(e.astype(BF16), v_ref[...].astype(BF16), preferred_element_type=F32)
    o_ref[...] = (o / l).astype(o_ref.dtype)


def attention_call(proj, b, t):
    return pl.pallas_call(
        _attn_kernel,
        grid=(b, C_HEADS),
        in_specs=[pl.BlockSpec((t, LANES), lambda bi, h: (bi, h)),
                  pl.BlockSpec((t, LANES), lambda bi, h: (bi, C_HEADS + h)),
                  pl.BlockSpec((t, LANES), lambda bi, h: (bi, 2 * C_HEADS + h))],
        out_specs=pl.BlockSpec((t, LANES), lambda bi, h: (bi, h)),
        out_shape=jax.ShapeDtypeStruct((b * t, C_WIDTH), BF16),
        compiler_params=_cparams(("arbitrary", "arbitrary")),
        name="ctx_attention",
    )(proj, proj, proj)


NA_QROWS = 8
NA_WROWS = 16
NA_CHUNK_ROWS = 4
NA_NCHUNK = NA_WROWS // NA_CHUNK_ROWS
NA_NSUB = 2


def _na_window_start(rb, rows):
    return np.clip(rb * NA_QROWS - NA_KH // 2, 0, rows - NA_WROWS)


def na_bias_tables(rpb, rows):
    n_rb = rows // NA_QROWS
    geos = []
    for rb in (0, 1, n_rb - 1):
        r0 = rb * NA_QROWS
        ws = _na_window_start(rb, rows)
        qr = r0 + np.arange(NA_QROWS)[:, None, None, None]
        qc = np.arange(GRID_W)[None, :, None, None]
        kr = ws + np.arange(NA_WROWS)[None, None, :, None]
        kc = np.arange(GRID_W)[None, None, None, :]
        rs = np.clip(qr - NA_KH // 2, 0, rows - NA_KH)
        cs = np.clip(qc - NA_KW // 2, 0, GRID_W - NA_KW)
        ok = (kr >= rs) & (kr < rs + NA_KH) & (kc >= cs) & (kc < cs + NA_KW)
        ri = np.clip(kr - qr + NA_KH - 1, 0, 2 * NA_KH - 2)
        ci = np.clip(kc - qc + NA_KW - 1, 0, 2 * NA_KW - 2)
        shape = (NA_QROWS, GRID_W, NA_WROWS, GRID_W)
        ok = np.broadcast_to(ok, shape).reshape(NA_QROWS * GRID_W, NA_WROWS * GRID_W)
        by_col = rpb[:, :, ci[0, :, 0, :]]
        tab = by_col[:, ri[:, 0, :, 0]]
        tab = tab.transpose(0, 1, 3, 2, 4).reshape((rpb.shape[0],) + ok.shape)
        geos.append(jnp.where(ok[None], tab, -jnp.inf))
    return jnp.stack(geos, axis=1).astype(F32)


def _na_kernel(*refs):
    q_ref = refs[0]
    k_refs = refs[1:1 + NA_NCHUNK]
    v_refs = refs[1 + NA_NCHUNK:1 + 2 * NA_NCHUNK]
    ck_ref, cv_ref, bias_ref, o_ref = refs[1 + 2 * NA_NCHUNK:]
    scale = C_HEAD_DIM ** -0.5
    k = jnp.concatenate([r[...].astype(BF16) for r in k_refs], axis=0)
    v = jnp.concatenate([r[...].astype(BF16) for r in v_refs], axis=0)
    ck = ck_ref[...].astype(BF16)
    cv = cv_ref[...].astype(BF16)
    ts = q_ref.shape[0] // NA_NSUB
    scored = []
    for a in range(NA_NSUB):
        rows = slice(a * ts, (a + 1) * ts)
        q = q_ref[rows, :].astype(BF16)
        s = lax.dot_general(q, k, NT_DIMS, preferred_element_type=F32) * scale + bias_ref[rows, :]
        sc = lax.dot_general(q, ck, NT_DIMS, preferred_element_type=F32) * scale
        mx = jnp.maximum(jnp.max(s, axis=-1, keepdims=True), jnp.max(sc, axis=-1, keepdims=True))
        scored.append((s, sc, mx))
    for a, (s, sc, mx) in enumerate(scored):
        e = jnp.exp(s - mx)
        ec = jnp.exp(sc - mx)
        l = jnp.sum(e, axis=-1, keepdims=True) + jnp.sum(ec, axis=-1, keepdims=True)
        o = (jnp.dot(e.astype(BF16), v, preferred_element_type=F32)
             + jnp.dot(ec.astype(BF16), cv, preferred_element_type=F32))
        o_ref[a * ts:(a + 1) * ts, :] = (o / l).astype(o_ref.dtype)


def na_call(proj, tok0, b, t, ck, cv, bias):
    rows = t // GRID_W
    n_rb = rows // NA_QROWS
    qtok = NA_QROWS * GRID_W
    ctok = NA_CHUNK_ROWS * GRID_W
    lc = ck.shape[1]
    q0 = tok0 // qtok
    c0 = tok0 // ctok
    cps = t // ctok

    def wchunk(rb):
        return jnp.clip(rb * (NA_QROWS // NA_CHUNK_ROWS) - 1, 0, (rows - NA_WROWS) // NA_CHUNK_ROWS)

    def geo(rb):
        return jnp.where(rb == 0, 0, jnp.where(rb == n_rb - 1, 2, 1))

    def kspec(c, colblk):
        return pl.BlockSpec((ctok, LANES),
                            lambda h, rb, bi: (c0 + bi * cps + wchunk(rb) + c, colblk + h))

    in_specs = ([pl.BlockSpec((qtok, LANES), lambda h, rb, bi: (q0 + bi * n_rb + rb, h))]
                + [kspec(c, C_HEADS) for c in range(NA_NCHUNK)]
                + [kspec(c, 2 * C_HEADS) for c in range(NA_NCHUNK)]
                + [pl.BlockSpec((None, lc, LANES), lambda h, rb, bi: (bi, 0, h)),
                   pl.BlockSpec((None, lc, LANES), lambda h, rb, bi: (bi, 0, h)),
                   pl.BlockSpec((None, None, qtok, NA_WROWS * GRID_W),
                                lambda h, rb, bi: (h, geo(rb), 0, 0))])
    return pl.pallas_call(
        _na_kernel,
        grid=(C_HEADS, n_rb, b),
        in_specs=in_specs,
        out_specs=pl.BlockSpec((qtok, LANES), lambda h, rb, bi: (bi * n_rb + rb, h)),
        out_shape=jax.ShapeDtypeStruct((b * t, C_WIDTH), BF16),
        compiler_params=_cparams(("arbitrary", "arbitrary", "arbitrary")),
        name="neighbourhood_attention",
    )(*([proj] * (1 + 2 * NA_NCHUNK)), ck, cv, bias)


LRU_HALO = SUBLANES


def _expm1(x):
    u = jnp.exp(x)
    return jnp.where(u == 1.0, x, (u - 1.0) * x / jnp.log(u))


def _lru_gates_kernel(x_ref, prev_ref, next_ref, cw_ref, cb_ref, wa_ref, ba_ref, wi_ref, bi_ref,
                      lam_ref, af_ref, bf_ref, ab_ref, bb_ref, *, tt, n_prompt, t_p, t_s):
    i = pl.program_id(0)
    s = i * tt
    in_prompt = s < n_prompt
    pos = jnp.where(in_prompt, s % t_p, (s - n_prompt) % t_s)
    seq_len = jnp.where(in_prompt, t_p, t_s)
    is_start = pos == 0
    is_end = pos + tt == seq_len
    x = x_ref[...]
    prev = jnp.where(is_start, 0.0, prev_ref[LRU_HALO - 2:LRU_HALO, :])
    nxt = jnp.where(is_end, 0.0, next_ref[0:1, :])
    xp = jnp.concatenate([prev, x, nxt], axis=0)
    cw = cw_ref[...]
    xc = cb_ref[...] + xp[0:tt] * cw[0:1]
    for j in range(1, CONV_W):
        xc = xc + xp[j:j + tt] * cw[j:j + 1]
    outs = ((af_ref, bf_ref), (ab_ref, bb_ref))
    for n in range(LRU_BLOCKS):
        sl = slice(n * LRU_BLOCK, (n + 1) * LRU_BLOCK)
        xb = xc[:, sl]
        xb16 = xb.astype(BF16)
        for dr in range(2):
            r = _sigmoid(jnp.dot(xb16, wa_ref[dr, n], preferred_element_type=F32) + ba_ref[dr:dr + 1, sl])
            g = _sigmoid(jnp.dot(xb16, wi_ref[dr, n], preferred_element_type=F32) + bi_ref[dr:dr + 1, sl])
            z = -lam_ref[dr:dr + 1, sl]
            softplus = jnp.maximum(z, 0.0) + jnp.log(1.0 + jnp.exp(-jnp.abs(z)))
            log_a = -LRU_C * r * softplus
            a_out, b_out = outs[dr]
            a_out[:, sl] = jnp.exp(log_a)
            b_out[:, sl] = jnp.sqrt(-_expm1(2.0 * log_a)) * g * xb


def lru_gates(proj, conv_w, conv_b, w_a, b_a, w_i, b_i, lam, *, n_prompt, t_p, t_s, tt=256):
    n = proj.shape[0]
    xcol = (3 * A_WIDTH) // LRU_WIDTH
    hb = tt // LRU_HALO
    nh = n // LRU_HALO
    full = lambda shape: pl.BlockSpec(shape, lambda i: (0,) * len(shape))
    out = jax.ShapeDtypeStruct((n, LRU_WIDTH), F32)
    ospec = pl.BlockSpec((tt, LRU_WIDTH), lambda i: (i, 0))
    return pl.pallas_call(
        functools.partial(_lru_gates_kernel, tt=tt, n_prompt=n_prompt, t_p=t_p, t_s=t_s),
        grid=(n // tt,),
        in_specs=[pl.BlockSpec((tt, LRU_WIDTH), lambda i: (i, xcol)),
                  pl.BlockSpec((LRU_HALO, LRU_WIDTH), lambda i: (jnp.maximum(i * hb - 1, 0), xcol)),
                  pl.BlockSpec((LRU_HALO, LRU_WIDTH), lambda i: (jnp.minimum((i + 1) * hb, nh - 1), xcol)),
                  full((CONV_W, LRU_WIDTH)), full((1, LRU_WIDTH)),
                  full((2, LRU_BLOCKS, LRU_BLOCK, LRU_BLOCK)), full((2, LRU_WIDTH)),
                  full((2, LRU_BLOCKS, LRU_BLOCK, LRU_BLOCK)), full((2, LRU_WIDTH)),
                  full((2, LRU_WIDTH))],
        out_specs=[ospec] * 4,
        out_shape=[out] * 4,
        compiler_params=_cparams(("arbitrary",)),
        name="lru_gates",
    )(proj, proj, proj, conv_w, conv_b.reshape(1, LRU_WIDTH), w_a.astype(BF16), b_a,
      w_i.astype(BF16), b_i, lam)


def _lru_scan_kernel(*refs, tt, reverse, combine):
    if combine:
        a_ref, b_ref, h0_ref, hf_ref, g_ref, o_ref, fin_ref, h_scr = refs
    else:
        a_ref, b_ref, h0_ref, o_ref, fin_ref, h_scr = refs
    ti = pl.program_id(1)

    @pl.when(ti == 0)
    def _():
        h_scr[...] = h0_ref[...]

    def step(j, h):
        t = tt - 1 - j if reverse else j
        h = a_ref[t] * h + b_ref[t]
        if combine:
            o_ref[t] = ((hf_ref[t] + h) * _gelu(g_ref[t])).astype(o_ref.dtype)
        else:
            o_ref[t] = h
        return h

    h = lax.fori_loop(0, tt, step, h_scr[...], unroll=8)
    h_scr[...] = h
    fin_ref[...] = h


def lru_scan(a, b, h0, tok0, nb, t, *, reverse, hf=None, proj=None, tt=256):
    tt = min(tt, t)
    nt = t // tt
    r0 = tok0 // tt
    combine = hf is not None
    vreg = (SUBLANES, LANES)
    view = lambda z: z.reshape(z.shape[0], LRU_WIDTH // LANES, LANES)

    def tmap(bi, ti):
        return nt - 1 - ti if reverse else ti

    in_specs = [pl.BlockSpec((tt,) + vreg, lambda bi, ti: (r0 + bi * nt + tmap(bi, ti), 0, 0)),
                pl.BlockSpec((tt,) + vreg, lambda bi, ti: (r0 + bi * nt + tmap(bi, ti), 0, 0)),
                pl.BlockSpec((None,) + vreg, lambda bi, ti: (bi, 0, 0))]
    args = [view(a), view(b), view(h0)]
    if combine:
        gcol = (3 * A_WIDTH + LRU_WIDTH) // LRU_WIDTH
        in_specs += [pl.BlockSpec((tt,) + vreg, lambda bi, ti: (bi * nt + tmap(bi, ti), 0, 0)),
                     pl.BlockSpec((tt,) + vreg, lambda bi, ti: (r0 + bi * nt + tmap(bi, ti), gcol, 0))]
        args += [view(hf), proj.reshape(proj.shape[0], EVEN_IN // LANES, LANES)]
    out, fin = pl.pallas_call(
        functools.partial(_lru_scan_kernel, tt=tt, reverse=reverse, combine=combine),
        grid=(nb, nt),
        in_specs=in_specs,
        out_specs=[pl.BlockSpec((tt,) + vreg, lambda bi, ti: (bi * nt + tmap(bi, ti), 0, 0)),
                   pl.BlockSpec((None,) + vreg, lambda bi, ti: (bi, 0, 0))],
        out_shape=[jax.ShapeDtypeStruct((nb * t,) + vreg, F32),
                   jax.ShapeDtypeStruct((nb,) + vreg, F32)],
        scratch_shapes=[pltpu.VMEM(vreg, F32)],
        compiler_params=_cparams(("arbitrary", "arbitrary")),
        name="lru_scan_bwd" if reverse else "lru_scan_fwd",
    )(*args)
    return out.reshape(nb * t, LRU_WIDTH), fin.reshape(nb, LRU_WIDTH)


def _topk_rows(s_list, k, payloads=None):
    n = s_list[0].shape[0]
    iota = lax.broadcasted_iota(jnp.int32, s_list[0].shape, 0).astype(F32)
    s_list = list(s_list)
    vals = [[] for _ in s_list]
    idxs = [[] for _ in s_list]
    for _ in range(k):
        for j, s in enumerate(s_list):
            m = jnp.max(s, axis=0, keepdims=True)
            am = jnp.min(jnp.where(s == m, iota, float(n)), axis=0, keepdims=True)
            sel = iota == am
            vals[j].append(m)
            if payloads is None:
                idxs[j].append(am)
            else:
                idxs[j].append(jnp.sum(jnp.where(sel, payloads[j], 0.0), axis=0, keepdims=True))
            s_list[j] = jnp.where(sel, -jnp.inf, s)
    return [(jnp.concatenate(v, axis=0), jnp.concatenate(i, axis=0)) for v, i in zip(vals, idxs)]


def _topk_rows_paired(s_list, k):
    n = s_list[0].shape[0]
    half = n // 2
    iota = lax.broadcasted_iota(jnp.int32, (half,) + s_list[0].shape[1:], 0).astype(F32)
    iota_b = iota + float(half)
    state = []
    for s in s_list:
        a, b = s[:half], s[half:]
        ge = a >= b
        state.append((jnp.where(ge, a, b), jnp.where(ge, b, a),
                      jnp.where(ge, iota, iota_b), jnp.where(ge, iota_b, iota)))
    vals = [[] for _ in s_list]
    idxs = [[] for _ in s_list]
    for _ in range(k):
        for j, (hi, lo, ih, il) in enumerate(state):
            m = jnp.max(hi, axis=0, keepdims=True)
            am = jnp.min(jnp.where(hi == m, ih, float(n)), axis=0, keepdims=True)
            sel = ih == am
            vals[j].append(m)
            idxs[j].append(am)
            state[j] = (jnp.where(sel, lo, hi), jnp.where(sel, -jnp.inf, lo),
                        jnp.where(sel, il, ih), il)
    return [(jnp.concatenate(v, axis=0), jnp.concatenate(i, axis=0)) for v, i in zip(vals, idxs)]


def _peer_route_kernel(q_ref, sk_ref, idx_ref, gate_ref, *, tm):
    half = PEER_QDIM // 2
    n_chunk = tm // LANES
    scores = []
    for c in range(n_chunk):
        q = q_ref[c * LANES:(c + 1) * LANES, :].astype(BF16)
        for p in range(2):
            scores.append(lax.dot_general(sk_ref[p], q[:, p * half:(p + 1) * half], NT_DIMS,
                                          preferred_element_type=F32))
    tops = _topk_rows_paired(scores, PEER_TOPK)
    nb = [PEER_TOPK // (a + 1) for a in range(PEER_TOPK)]
    pad = -sum(nb) % SUBLANES
    cands, cidxs = [], []
    for c in range(n_chunk):
        (s1, i1), (s2, i2) = tops[2 * c], tops[2 * c + 1]
        cands.append(jnp.concatenate([s1[a:a + 1] + s2[:nb[a]] for a in range(PEER_TOPK)]
                                     + [jnp.full((pad, LANES), -jnp.inf, F32)], axis=0))
        cidxs.append(jnp.concatenate(
            [i1[a:a + 1] * float(PEER_NKEYS) + i2[:nb[a]] for a in range(PEER_TOPK)]
            + [jnp.zeros((pad, LANES), F32)], axis=0))
    best = _topk_rows(cands, PEER_TOPK, payloads=cidxs)
    for c, (bs, bidx) in enumerate(best):
        e = jnp.exp(bs - bs[0:1])
        gates = e / jnp.sum(e, axis=0, keepdims=True)
        idx_ref[:, c * LANES:(c + 1) * LANES] = bidx.astype(jnp.int32)
        gate_ref[:, c * LANES:(c + 1) * LANES] = gates


def peer_route(q, sub_keys, tm=512):
    n = q.shape[0]
    half = PEER_QDIM // 2
    return pl.pallas_call(
        functools.partial(_peer_route_kernel, tm=tm),
        grid=(n // tm, PEER_HEADS),
        in_specs=[pl.BlockSpec((tm, PEER_QDIM), lambda i, h: (i, h)),
                  pl.BlockSpec((None, 2, PEER_NKEYS, half), lambda i, h: (h, 0, 0, 0))],
        out_specs=[pl.BlockSpec((None, PEER_TOPK, tm), lambda i, h: (h, 0, i))] * 2,
        out_shape=[jax.ShapeDtypeStruct((PEER_HEADS, PEER_TOPK, n), jnp.int32),
                   jax.ShapeDtypeStruct((PEER_HEADS, PEER_TOPK, n), F32)],
        compiler_params=_cparams(("arbitrary", "arbitrary")),
        name="peer_route",
    )(q, sub_keys)


def _peer_gates_kernel(idx_ref, gate_ref, g_ref):
    idx = idx_ref[...]
    gate = gate_ref[...]
    tg, _, hk = idx.shape
    i1 = (idx // PEER_NKEYS).astype(F32).astype(BF16)
    i2 = (idx % PEER_NKEYS).astype(F32).astype(BF16)
    iota = lax.broadcasted_iota(jnp.int32, (1, PEER_NKEYS, hk), 1).astype(F32).astype(BF16)
    zero = jnp.zeros((), BF16)
    a = jnp.where(i1 == iota, gate.astype(BF16), zero)
    bsel = jnp.where(i2 == iota, jnp.ones((), BF16), zero)
    g3 = lax.dot_general(a, bsel, (((2,), (2,)), ((0,), (0,))), preferred_element_type=F32)
    g_ref[...] = pltpu.einshape("nij->inj", g3).astype(g_ref.dtype)


def peer_gate_rows(idx, gates, tg=64):
    n, hk = idx.shape
    return pl.pallas_call(
        _peer_gates_kernel,
        grid=(n // tg,),
        in_specs=[pl.BlockSpec((tg, 1, hk), lambda i: (i, 0, 0))] * 2,
        out_specs=pl.BlockSpec((PEER_NKEYS, tg, PEER_NKEYS), lambda i: (0, i, 0)),
        out_shape=jax.ShapeDtypeStruct((PEER_NKEYS, n, PEER_NKEYS), BF16),
        compiler_params=_cparams(("arbitrary",)),
        name="peer_gate_rows",
    )(idx.reshape(n, 1, hk), gates.reshape(n, 1, hk))


def _peer_dense_kernel(h_ref, ut_ref, v_ref, gr_ref, x_ref, mg_ref, fg_ref, o_ref, *, final_norm,
                       row_split):
    e = pl.program_id(1)

    @pl.when(e == 0)
    def _():
        o_ref[...] = jnp.zeros_like(o_ref)

    tr = h_ref.shape[0] // row_split
    scores = [jnp.dot(h_ref[r * tr:(r + 1) * tr, :], ut_ref[...], preferred_element_type=F32)
              for r in range(row_split)]
    for r, s in enumerate(scores):
        rows = slice(r * tr, (r + 1) * tr)
        act = jnp.concatenate(
            [(_gelu(s[:, j * PEER_NKEYS:(j + 1) * PEER_NKEYS])
              * gr_ref[j, rows, :].astype(F32)).astype(BF16)
             for j in range(gr_ref.shape[0])], axis=1)
        o_ref[rows, :] += jnp.dot(act, v_ref[...], preferred_element_type=F32)

    @pl.when(e == pl.num_programs(1) - 1)
    def _():
        y = x_ref[...] + mg_ref[...] * o_ref[...]
        if final_norm:
            y = y * lax.rsqrt(jnp.mean(y * y, axis=-1, keepdims=True) + EPS) * fg_ref[...]
        o_ref[...] = y


def peer_dense(h, ut16, v16, grows, x, mod, gate_idx, final_g, *, n_prompt, t_s, final_norm,
               tm=1024, te=1024, row_split=2):
    n, d = x.shape
    ne = v16.shape[0]
    row = functools.partial(_mod_row, tm=tm, n_prompt=n_prompt, t_s=t_s)
    return pl.pallas_call(
        functools.partial(_peer_dense_kernel, final_norm=final_norm, row_split=row_split),
        grid=(n // tm, ne // te),
        in_specs=[pl.BlockSpec((tm, d), lambda i, e: (i, 0)),
                  pl.BlockSpec((d, te), lambda i, e: (0, e)),
                  pl.BlockSpec((te, d), lambda i, e: (e, 0)),
                  pl.BlockSpec((te // PEER_NKEYS, tm, PEER_NKEYS), lambda i, e: (e, i, 0)),
                  pl.BlockSpec((tm, d), lambda i, e: (i, 0), pipeline_mode=pl.Buffered(1)),
                  pl.BlockSpec((None, None, 1, d), lambda i, e: (row(i), gate_idx, 0, 0)),
                  pl.BlockSpec((1, d), lambda i, e: (0, 0))],
        out_specs=pl.BlockSpec((tm, d), lambda i, e: (i, 0), pipeline_mode=pl.Buffered(1)),
        out_shape=jax.ShapeDtypeStruct((n, d), F32),
        compiler_params=_cparams(("arbitrary", "arbitrary")),
        name="peer_dense",
    )(h, ut16, v16, grows, x, mod, final_g.reshape(1, d))


def peer_block(x, norm_g, mod, w_q16, sk16, ut16, v16, final_g, *, n_prompt, t_s, final_norm):
    n = x.shape[0]
    q, h = norm_matmul(x, norm_g, mod, 3, 4, w_q16, n_prompt=n_prompt, t_s=t_s, emit_h=True)
    idx, gates = peer_route(q, sk16)
    hk = PEER_HEADS * PEER_TOPK
    idx = idx.reshape(hk, n).T
    gates = gates.reshape(hk, n).T
    grows = peer_gate_rows(idx, gates)
    return peer_dense(h, ut16, v16, grows, x, mod, 5, final_g, n_prompt=n_prompt, t_s=t_s,
                      final_norm=final_norm)


def kernel(x_prompt, x_sample, cache_a_k, cache_a_v, state_lru, cache_c_k, cache_c_v, c, c_ctx, w_mod, b_mod, norm1_g, norm2_g, final_norm_g, even_w_in, even_w_out, a_lambda, a_subln_g, lru_conv_w, lru_conv_b, lru_w_a, lru_b_a, lru_w_i, lru_b_i, lru_lambda, odd_w_in, odd_w_out, na_rpb, peer_w_q, peer_sub_keys, peer_u, peer_v):
    b_p, t_p, d = x_prompt.shape
    b_s, t_s, _ = x_sample.shape
    depth = w_mod.shape[0]
    n_p = b_p * t_p
    n_s = b_s * t_s
    lc = cache_a_k.shape[2]
    tok = dict(n_prompt=n_p, t_s=t_s)

    x = jnp.concatenate([x_prompt.reshape(n_p, d), x_sample.reshape(n_s, d)], axis=0)
    n_rows = -(-(1 + b_s) // SUBLANES) * SUBLANES
    cond = jnp.concatenate([c_ctx[None], c, jnp.zeros((n_rows - 1 - b_s, d), F32)], axis=0)
    mod_all = modulation_all(cond, w_mod, b_mod).reshape(depth, n_rows, 6, 1, d)
    rope = rope_tables(t_s)

    new_ak, new_av, new_lru, new_ck, new_cv = [], [], [], [], []
    for l in range(depth):
        mod = mod_all[l]
        if l % 2 == 0:
            e = l // 2
            lam_init = 0.8 - 0.6 * math.exp(-0.3 * l)
            proj = norm_matmul(x, norm1_g[l], mod, 0, 1, even_w_in[e].astype(BF16), **tok)
            ctx = (cache_a_k[:, e].reshape(b_s, lc, A_WIDTH), cache_a_v[:, e].reshape(b_s, lc, A_WIDTH))
            oa_p = diff_attention_call(proj, 0, b_p, t_p, a_lambda[e], a_subln_g[e], lam_init)
            oa_s = diff_attention_call(proj, n_p, b_s, t_s, a_lambda[e], a_subln_g[e], lam_init,
                                       ctx=ctx, rope=rope)
            a_f, b_f, a_b, b_b = lru_gates(proj, lru_conv_w[e], lru_conv_b[e], lru_w_a[e], lru_b_a[e],
                                           lru_w_i[e], lru_b_i[e], lru_lambda[e],
                                           n_prompt=n_p, t_p=t_p, t_s=t_s)
            zeros_p = jnp.zeros((b_p, LRU_WIDTH), F32)
            hf_p, fin_f = lru_scan(a_f, b_f, zeros_p, 0, b_p, t_p, reverse=False)
            ob_p, fin_b = lru_scan(a_b, b_b, zeros_p, 0, b_p, t_p, reverse=True, hf=hf_p, proj=proj)
            hf_s, _ = lru_scan(a_f, b_f, state_lru[:, e, 0], n_p, b_s, t_s, reverse=False)
            ob_s, _ = lru_scan(a_b, b_b, state_lru[:, e, 1], n_p, b_s, t_s, reverse=True,
                               hf=hf_s, proj=proj)
            w_out = even_w_out[e].astype(BF16)
            x = matmul_res([(oa_p, oa_s), (ob_p, ob_s)], [w_out[:A_WIDTH], w_out[A_WIDTH:]],
                           x, mod, 2, **tok)
            new_ak.append(proj[:n_p, A_WIDTH:2 * A_WIDTH].reshape(b_p, t_p, A_HEADS, 2 * A_QK_DIM))
            new_av.append(proj[:n_p, 2 * A_WIDTH:3 * A_WIDTH].reshape(b_p, t_p, A_HEADS, A_V_DIM))
            new_lru.append(jnp.stack([fin_f, fin_b], axis=1))
        else:
            o = l // 2
            proj = norm_matmul(x, norm1_g[l], mod, 0, 1, odd_w_in[o].astype(BF16), **tok)
            oc_p = attention_call(proj, b_p, t_p)
            bias = na_bias_tables(na_rpb[o], t_s // GRID_W)
            oc_s = na_call(proj, n_p, b_s, t_s, cache_c_k[:, o].reshape(b_s, lc, C_WIDTH),
                           cache_c_v[:, o].reshape(b_s, lc, C_WIDTH), bias)
            x = matmul_res([(oc_p, oc_s)], [odd_w_out[o].astype(BF16)], x, mod, 2, **tok)
            new_ck.append(proj[:n_p, C_WIDTH:2 * C_WIDTH].reshape(b_p, t_p, C_HEADS, C_HEAD_DIM))
            new_cv.append(proj[:n_p, 2 * C_WIDTH:3 * C_WIDTH].reshape(b_p, t_p, C_HEADS, C_HEAD_DIM))
        x = peer_block(x, norm2_g[l], mod, peer_w_q[l].astype(BF16), peer_sub_keys[l].astype(BF16),
                       peer_u[l].astype(BF16).T, peer_v[l].astype(BF16), final_norm_g,
                       final_norm=(l == depth - 1), **tok)
    y_prompt = x[:n_p].reshape(b_p, t_p, d)
    y_sample = x[n_p:].reshape(b_s, t_s, d)
    return (y_prompt, y_sample, jnp.stack(new_ak, axis=1), jnp.stack(new_av, axis=1),
            jnp.stack(new_lru, axis=1), jnp.stack(new_ck, axis=1), jnp.stack(new_cv, axis=1))
```

```python
import functools
import math

import numpy as np
import jax
import jax.numpy as jnp
from jax import lax
from jax.experimental import pallas as pl
from jax.experimental.pallas import tpu as pltpu

F32 = jnp.float32
BF16 = jnp.bfloat16

D_MODEL = 2048
GRID_W = 64
EPS = 1e-6
ROPE_BASE = 10000.0
A_HEADS = 8
A_QK_DIM = 64
A_V_DIM = 128
A_WIDTH = A_HEADS * A_V_DIM
LRU_WIDTH = 1024
LRU_BLOCKS = 8
LRU_BLOCK = LRU_WIDTH // LRU_BLOCKS
CONV_W = 4
LRU_C = 8.0
EVEN_IN = 3 * A_WIDTH + 2 * LRU_WIDTH
C_HEADS = 16
C_HEAD_DIM = 128
C_WIDTH = C_HEADS * C_HEAD_DIM
NA_KH = 8
NA_KW = 16
PEER_HEADS = 8
PEER_NKEYS = 128
PEER_N = PEER_NKEYS * PEER_NKEYS
PEER_QDIM = 256
PEER_TOPK = 16

LANES = 128
SUBLANES = 8
VMEM_LIMIT = 60 * 1024 * 1024

NT_DIMS = (((1,), (1,)), ((), ()))


def _cparams(sem):
    return pltpu.CompilerParams(dimension_semantics=sem, vmem_limit_bytes=VMEM_LIMIT)


def _gelu(x):
    c = math.sqrt(2.0 / math.pi)
    return 0.5 * x * (1.0 + jnp.tanh(c * (x + 0.044715 * (x * x * x))))


def _sigmoid(x):
    return 1.0 / (1.0 + jnp.exp(-x))


def _mod_kernel(c_ref, w_ref, b_ref, o_ref):
    c = c_ref[...]
    s = c * _sigmoid(c)
    o_ref[0] = jnp.dot(s.astype(BF16), w_ref[0].astype(BF16),
                       preferred_element_type=F32) + b_ref[0]


def modulation_all(cond, w_mod, b_mod):
    depth, d, n6 = w_mod.shape
    r = cond.shape[0]
    tn = 768
    return pl.pallas_call(
        _mod_kernel,
        grid=(depth, n6 // tn),
        in_specs=[pl.BlockSpec((r, d), lambda l, j: (0, 0)),
                  pl.BlockSpec((1, d, tn), lambda l, j: (l, 0, j)),
                  pl.BlockSpec((1, 1, tn), lambda l, j: (l, 0, j))],
        out_specs=pl.BlockSpec((1, r, tn), lambda l, j: (l, 0, j)),
        out_shape=jax.ShapeDtypeStruct((depth, r, n6), F32),
        compiler_params=_cparams(("arbitrary", "arbitrary")),
        name="modulation",
    )(cond, w_mod, b_mod.reshape(depth, 1, n6))


def _mod_row(i, tm, n_prompt, t_s):
    s = i * tm
    return jnp.where(s < n_prompt, 0, 1 + (s - n_prompt) // t_s)


def _norm_matmul_kernel(x_ref, g_ref, sh_ref, sc_ref, w_ref, o_ref, *rest, emit_h):
    if emit_h:
        h_out_ref, h_scr = rest
    else:
        (h_scr,) = rest

    @pl.when(pl.program_id(1) == 0)
    def _():
        x = x_ref[...]
        y = x * lax.rsqrt(jnp.mean(x * x, axis=-1, keepdims=True) + EPS)
        h = (y * g_ref[...]) * (1.0 + sc_ref[...]) + sh_ref[...]
        h_scr[...] = h.astype(BF16)
        if emit_h:
            h_out_ref[...] = h.astype(BF16)

    o_ref[...] = jnp.dot(h_scr[...], w_ref[...], preferred_element_type=F32).astype(o_ref.dtype)


def norm_matmul(x, g, mod, shift_idx, scale_idx, w, *, n_prompt, t_s, emit_h=False,
                tm=1024, tn=1024, out_dtype=F32):
    n, d = x.shape
    nout = w.shape[1]
    row = functools.partial(_mod_row, tm=tm, n_prompt=n_prompt, t_s=t_s)
    out_shape = [jax.ShapeDtypeStruct((n, nout), out_dtype)]
    out_specs = [pl.BlockSpec((tm, tn), lambda i, j: (i, j))]
    if emit_h:
        out_shape.append(jax.ShapeDtypeStruct((n, d), BF16))
        out_specs.append(pl.BlockSpec((tm, d), lambda i, j: (i, 0)))
    res = pl.pallas_call(
        functools.partial(_norm_matmul_kernel, emit_h=emit_h),
        grid=(n // tm, nout // tn),
        in_specs=[pl.BlockSpec((tm, d), lambda i, j: (i, 0)),
                  pl.BlockSpec((1, d), lambda i, j: (0, 0)),
                  pl.BlockSpec((None, None, 1, d), lambda i, j: (row(i), shift_idx, 0, 0)),
                  pl.BlockSpec((None, None, 1, d), lambda i, j: (row(i), scale_idx, 0, 0)),
                  pl.BlockSpec((d, tn), lambda i, j: (0, j))],
        out_specs=out_specs,
        out_shape=out_shape,
        scratch_shapes=[pltpu.VMEM((tm, d), BF16)],
        compiler_params=_cparams(("arbitrary", "arbitrary")),
        name="norm_matmul",
    )(x, g.reshape(1, d), mod, mod, w)
    return res if emit_h else res[0]


def _matmul_res_kernel(*refs, n_in, np_tiles):
    ap_refs = refs[:n_in]
    as_refs = refs[n_in:2 * n_in]
    w_refs = refs[2 * n_in:3 * n_in]
    x_ref, gate_ref, o_ref = refs[3 * n_in:]

    def emit(a_refs):
        acc = None
        for a_ref, w_ref in zip(a_refs, w_refs):
            p = jnp.dot(a_ref[...].astype(BF16), w_ref[...], preferred_element_type=F32)
            acc = p if acc is None else acc + p
        o_ref[...] = x_ref[...] + gate_ref[...] * acc

    i = pl.program_id(0)
    pl.when(i < np_tiles)(lambda: emit(ap_refs))
    pl.when(i >= np_tiles)(lambda: emit(as_refs))


def matmul_res(a_pairs, w_list, x, mod, gate_idx, *, n_prompt, t_s, tm=1024, tn=512):
    n, d = x.shape
    n_in = len(a_pairs)
    np_tiles = n_prompt // tm
    ns_tiles = n // tm - np_tiles
    row = functools.partial(_mod_row, tm=tm, n_prompt=n_prompt, t_s=t_s)
    in_specs = ([pl.BlockSpec((tm, ap.shape[1]), lambda i, j: (jnp.minimum(i, np_tiles - 1), 0))
                 for ap, _ in a_pairs]
                + [pl.BlockSpec((tm, a_s.shape[1]),
                                lambda i, j: (jnp.clip(i - np_tiles, 0, ns_tiles - 1), 0))
                   for _, a_s in a_pairs]
                + [pl.BlockSpec((w.shape[0], tn), lambda i, j: (0, j)) for w in w_list]
                + [pl.BlockSpec((tm, tn), lambda i, j: (i, j)),
                   pl.BlockSpec((None, None, 1, tn), lambda i, j: (row(i), gate_idx, 0, j))])
    return pl.pallas_call(
        functools.partial(_matmul_res_kernel, n_in=n_in, np_tiles=np_tiles),
        grid=(n // tm, d // tn),
        in_specs=in_specs,
        out_specs=pl.BlockSpec((tm, tn), lambda i, j: (i, j)),
        out_shape=jax.ShapeDtypeStruct((n, d), F32),
        compiler_params=_cparams(("arbitrary", "arbitrary")),
        name="matmul_res",
    )(*[ap for ap, _ in a_pairs], *[a_s for _, a_s in a_pairs], *w_list, x, mod)


def _rope(x, cos, sin_signed):
    lane = lax.broadcasted_iota(jnp.int32, x.shape, 1)
    first = (lane % 32) < 16
    partner = jnp.where(first, pltpu.roll(x, LANES - 16, axis=1), pltpu.roll(x, 16, axis=1))
    return x * cos + partner * sin_signed


def _diff_attn_kernel(*refs, has_ctx, lam_init, tq, n_sub):
    if has_ctx:
        (q_ref, k_ref, v_ref, ck_ref, cv_ref, cos_ref, sin_ref, lam_ref, g_ref,
         o_ref, k_scr, v_scr, ck_scr, cv_scr) = refs
    else:
        q_ref, k_ref, v_ref, lam_ref, g_ref, o_ref, k_scr, v_scr = refs
    qi = pl.program_id(2)

    @pl.when(qi == 0)
    def _():
        k = k_ref[...]
        if has_ctx:
            k = _rope(k, cos_ref[...], sin_ref[...])
            ck_scr[...] = ck_ref[...].astype(BF16)
            cv_scr[...] = cv_ref[...].astype(BF16)
        k_scr[...] = k.astype(BF16)
        v_scr[...] = v_ref[...].astype(BF16)

    q = q_ref[...]
    if has_ctx:
        off = pl.multiple_of(qi * tq, tq)
        q = _rope(q, cos_ref[pl.ds(off, tq), :], sin_ref[pl.ds(off, tq), :])
    q = q * (A_QK_DIM ** -0.5)
    lp = lam_ref[...]
    lam = (jnp.exp(jnp.sum(lp[0:1] * lp[1:2], axis=-1, keepdims=True))
           - jnp.exp(jnp.sum(lp[2:3] * lp[3:4], axis=-1, keepdims=True)) + lam_init)
    ts = tq // n_sub
    lane = lax.broadcasted_iota(jnp.int32, (ts, LANES), 1)

    scored = []
    for a in range(n_sub):
        qa = q[a * ts:(a + 1) * ts]
        q2 = jnp.concatenate([jnp.where(lane < A_QK_DIM, qa, 0.0),
                              jnp.where(lane >= A_QK_DIM, qa, 0.0)], axis=0).astype(BF16)
        s = lax.dot_general(q2, k_scr[...], NT_DIMS, preferred_element_type=F32)
        mx = jnp.max(s, axis=-1, keepdims=True)
        sc = None
        if has_ctx:
            sc = lax.dot_general(q2, ck_scr[...], NT_DIMS, preferred_element_type=F32)
            mx = jnp.maximum(mx, jnp.max(sc, axis=-1, keepdims=True))
        scored.append((s, sc, mx))
    for a, (s, sc, mx) in enumerate(scored):
        e = jnp.exp(s - mx)
        l = jnp.sum(e, axis=-1, keepdims=True)
        if has_ctx:
            ec = jnp.exp(sc - mx)
            l = l + jnp.sum(ec, axis=-1, keepdims=True)
        r = 1.0 / l
        r0 = r[:ts]
        r1 = lam * r[ts:]
        d = jnp.dot((e[:ts] * r0 - e[ts:] * r1).astype(BF16), v_scr[...],
                    preferred_element_type=F32)
        if has_ctx:
            d = d + jnp.dot((ec[:ts] * r0 - ec[ts:] * r1).astype(BF16), cv_scr[...],
                            preferred_element_type=F32)
        y = d * lax.rsqrt(jnp.mean(d * d, axis=-1, keepdims=True) + EPS)
        o_ref[a * ts:(a + 1) * ts, :] = ((y * g_ref[...]) * (1.0 - lam_init)).astype(o_ref.dtype)


def diff_attention_call(proj, tok0, b, t, lam_p, subln_g, lam_init, ctx=None, rope=None, tq=512,
                        n_sub=4):
    tq = min(tq, t)
    nq = t // tq
    q0 = tok0 // tq
    k0 = tok0 // t
    has_ctx = ctx is not None
    in_specs = [pl.BlockSpec((tq, LANES), lambda bi, h, qi: (q0 + bi * nq + qi, h)),
                pl.BlockSpec((t, LANES), lambda bi, h, qi: (k0 + bi, A_HEADS + h)),
                pl.BlockSpec((t, LANES), lambda bi, h, qi: (k0 + bi, 2 * A_HEADS + h))]
    args = [proj, proj, proj]
    if has_ctx:
        ck, cv = ctx
        lc = ck.shape[1]
        cos, sin = rope
        in_specs += [pl.BlockSpec((None, lc, LANES), lambda bi, h, qi: (bi, 0, h)),
                     pl.BlockSpec((None, lc, LANES), lambda bi, h, qi: (bi, 0, h)),
                     pl.BlockSpec((t, LANES), lambda bi, h, qi: (0, 0)),
                     pl.BlockSpec((t, LANES), lambda bi, h, qi: (0, 0))]
        args += [ck, cv, cos, sin]
    in_specs += [pl.BlockSpec((4, A_QK_DIM), lambda bi, h, qi: (0, 0)),
                 pl.BlockSpec((1, A_V_DIM), lambda bi, h, qi: (0, 0))]
    args += [lam_p, subln_g.reshape(1, A_V_DIM)]
    scratch = [pltpu.VMEM((t, LANES), BF16), pltpu.VMEM((t, LANES), BF16)]
    if has_ctx:
        scratch += [pltpu.VMEM((lc, LANES), BF16), pltpu.VMEM((lc, LANES), BF16)]
    return pl.pallas_call(
        functools.partial(_diff_attn_kernel, has_ctx=has_ctx, lam_init=lam_init, tq=tq, n_sub=n_sub),
        grid=(b, A_HEADS, nq),
        in_specs=in_specs,
        out_specs=pl.BlockSpec((tq, LANES), lambda bi, h, qi: (bi * nq + qi, h)),
        out_shape=jax.ShapeDtypeStruct((b * t, A_WIDTH), BF16),
        scratch_shapes=scratch,
        compiler_params=_cparams(("arbitrary", "arbitrary", "arbitrary")),
        name="diff_attention",
    )(*args)


def rope_tables(t):
    tok = jnp.arange(t)
    row = (tok // GRID_W).astype(F32)
    col = (tok % GRID_W).astype(F32)
    n_freq = A_QK_DIM // 4
    inv = ROPE_BASE ** (-jnp.arange(n_freq, dtype=F32) / n_freq)
    ang_r = row[:, None] * inv
    ang_c = col[:, None] * inv
    cos64 = jnp.concatenate([jnp.cos(ang_r), jnp.cos(ang_r), jnp.cos(ang_c), jnp.cos(ang_c)], axis=1)
    sin64 = jnp.concatenate([-jnp.sin(ang_r), jnp.sin(ang_r), -jnp.sin(ang_c), jnp.sin(ang_c)], axis=1)
    return jnp.tile(cos64, (1, 2)), jnp.tile(sin64, (1, 2))


def _attn_kernel(q_ref, k_ref, v_ref, o_ref):
    for h in range(C_HEADS):
        cols = slice(h * C_HEAD_DIM, (h + 1) * C_HEAD_DIM)
        q = q_ref[:, cols].astype(BF16)
        k = k_ref[:, cols].astype(BF16)
        s = lax.dot_general(q, k, NT_DIMS, preferred_element_type=F32) * (C_HEAD_DIM ** -0.5)
        mx = jnp.max(s, axis=-1, keepdims=True)
        e = jnp.exp(s - mx)
        l = jnp.sum(e, axis=-1, keepdims=True)
        o = jnp.dot(e.astype(BF16), v_ref[:, cols].astype(BF16), preferred_element_type=F32)
        o_ref[:, cols] = (o / l).astype(o_ref.dtype)


def attention_call(proj, b, t):
    return pl.pallas_call(
        _attn_kernel,
        grid=(b,),
        in_specs=[pl.BlockSpec((t, C_WIDTH), lambda bi: (bi, 0)),
                  pl.BlockSpec((t, C_WIDTH), lambda bi: (bi, 1)),
                  pl.BlockSpec((t, C_WIDTH), lambda bi: (bi, 2))],
        out_specs=pl.BlockSpec((t, C_WIDTH), lambda bi: (bi, 0)),
        out_shape=jax.ShapeDtypeStruct((b * t, C_WIDTH), BF16),
        compiler_params=_cparams(("arbitrary",)),
        name="ctx_attention",
    )(proj, proj, proj)


NA_QROWS = 8
NA_WROWS = 16
NA_CHUNK_ROWS = 4
NA_NCHUNK = NA_WROWS // NA_CHUNK_ROWS
NA_NSUB = 2


def _na_window_start(rb, rows):
    return np.clip(rb * NA_QROWS - NA_KH // 2, 0, rows - NA_WROWS)


def na_bias_tables(rpb, rows):
    n_rb = rows // NA_QROWS
    geos = []
    for rb in (0, 1, n_rb - 1):
        r0 = rb * NA_QROWS
        ws = _na_window_start(rb, rows)
        qr = r0 + np.arange(NA_QROWS)[:, None, None, None]
        qc = np.arange(GRID_W)[None, :, None, None]
        kr = ws + np.arange(NA_WROWS)[None, None, :, None]
        kc = np.arange(GRID_W)[None, None, None, :]
        rs = np.clip(qr - NA_KH // 2, 0, rows - NA_KH)
        cs = np.clip(qc - NA_KW // 2, 0, GRID_W - NA_KW)
        ok = (kr >= rs) & (kr < rs + NA_KH) & (kc >= cs) & (kc < cs + NA_KW)
        ri = np.clip(kr - qr + NA_KH - 1, 0, 2 * NA_KH - 2)
        ci = np.clip(kc - qc + NA_KW - 1, 0, 2 * NA_KW - 2)
        shape = (NA_QROWS, GRID_W, NA_WROWS, GRID_W)
        ok = np.broadcast_to(ok, shape).reshape(NA_QROWS * GRID_W, NA_WROWS * GRID_W)
        by_col = rpb[:, :, ci[0, :, 0, :]]
        tab = by_col[:, ri[:, 0, :, 0]]
        tab = tab.transpose(0, 1, 3, 2, 4).reshape((rpb.shape[0],) + ok.shape)
        geos.append(jnp.where(ok[None], tab, -jnp.inf))
    return jnp.stack(geos, axis=1).astype(F32)


def _na_kernel(*refs):
    q_ref = refs[0]
    k_refs = refs[1:1 + NA_NCHUNK]
    v_refs = refs[1 + NA_NCHUNK:1 + 2 * NA_NCHUNK]
    ck_ref, cv_ref, bias_ref, o_ref = refs[1 + 2 * NA_NCHUNK:]
    scale = C_HEAD_DIM ** -0.5
    k = jnp.concatenate([r[...].astype(BF16) for r in k_refs], axis=0)
    v = jnp.concatenate([r[...].astype(BF16) for r in v_refs], axis=0)
    ck = ck_ref[...].astype(BF16)
    cv = cv_ref[...].astype(BF16)
    ts = q_ref.shape[0] // NA_NSUB
    scored = []
    for a in range(NA_NSUB):
        rows = slice(a * ts, (a + 1) * ts)
        q = q_ref[rows, :].astype(BF16)
        s = lax.dot_general(q, k, NT_DIMS, preferred_element_type=F32) * scale + bias_ref[rows, :]
        sc = lax.dot_general(q, ck, NT_DIMS, preferred_element_type=F32) * scale
        mx = jnp.maximum(jnp.max(s, axis=-1, keepdims=True), jnp.max(sc, axis=-1, keepdims=True))
        scored.append((s, sc, mx))
    for a, (s, sc, mx) in enumerate(scored):
        e = jnp.exp(s - mx)
        ec = jnp.exp(sc - mx)
        l = jnp.sum(e, axis=-1, keepdims=True) + jnp.sum(ec, axis=-1, keepdims=True)
        o = (jnp.dot(e.astype(BF16), v, preferred_element_type=F32)
             + jnp.dot(ec.astype(BF16), cv, preferred_element_type=F32))
        o_ref[a * ts:(a + 1) * ts, :] = (o / l).astype(o_ref.dtype)


def na_call(proj, tok0, b, t, ck, cv, bias):
    rows = t // GRID_W
    n_rb = rows // NA_QROWS
    qtok = NA_QROWS * GRID_W
    ctok = NA_CHUNK_ROWS * GRID_W
    lc = ck.shape[1]
    q0 = tok0 // qtok
    c0 = tok0 // ctok
    cps = t // ctok

    def wchunk(rb):
        return jnp.clip(rb * (NA_QROWS // NA_CHUNK_ROWS) - 1, 0, (rows - NA_WROWS) // NA_CHUNK_ROWS)

    def geo(rb):
        return jnp.where(rb == 0, 0, jnp.where(rb == n_rb - 1, 2, 1))

    def kspec(c, colblk):
        return pl.BlockSpec((ctok, LANES),
                            lambda h, rb, bi: (c0 + bi * cps + wchunk(rb) + c, colblk + h))

    in_specs = ([pl.BlockSpec((qtok, LANES), lambda h, rb, bi: (q0 + bi * n_rb + rb, h))]
                + [kspec(c, C_HEADS) for c in range(NA_NCHUNK)]
                + [kspec(c, 2 * C_HEADS) for c in range(NA_NCHUNK)]
                + [pl.BlockSpec((None, lc, LANES), lambda h, rb, bi: (bi, 0, h)),
                   pl.BlockSpec((None, lc, LANES), lambda h, rb, bi: (bi, 0, h)),
                   pl.BlockSpec((None, None, qtok, NA_WROWS * GRID_W),
                                lambda h, rb, bi: (h, geo(rb), 0, 0))])
    return pl.pallas_call(
        _na_kernel,
        grid=(C_HEADS, n_rb, b),
        in_specs=in_specs,
        out_specs=pl.BlockSpec((qtok, LANES), lambda h, rb, bi: (bi * n_rb + rb, h)),
        out_shape=jax.ShapeDtypeStruct((b * t, C_WIDTH), BF16),
        compiler_params=_cparams(("arbitrary", "arbitrary", "arbitrary")),
        name="neighbourhood_attention",
    )(*([proj] * (1 + 2 * NA_NCHUNK)), ck, cv, bias)


LRU_HALO = SUBLANES


def _expm1(x):
    u = jnp.exp(x)
    return jnp.where(u == 1.0, x, (u - 1.0) * x / jnp.log(u))


def _lru_gates_kernel(x_ref, prev_ref, next_ref, cw_ref, cb_ref, wa_ref, ba_ref, wi_ref, bi_ref,
                      lam_ref, af_ref, bf_ref, ab_ref, bb_ref, *, tt, n_prompt, t_p, t_s):
    i = pl.program_id(0)
    s = i * tt
    in_prompt = s < n_prompt
    pos = jnp.where(in_prompt, s % t_p, (s - n_prompt) % t_s)
    seq_len = jnp.where(in_prompt, t_p, t_s)
    is_start = pos == 0
    is_end = pos + tt == seq_len
    x = x_ref[...]
    prev = jnp.where(is_start, 0.0, prev_ref[LRU_HALO - 2:LRU_HALO, :])
    nxt = jnp.where(is_end, 0.0, next_ref[0:1, :])
    xp = jnp.concatenate([prev, x, nxt], axis=0)
    cw = cw_ref[...]
    xc = cb_ref[...] + xp[0:tt] * cw[0:1]
    for j in range(1, CONV_W):
        xc = xc + xp[j:j + tt] * cw[j:j + 1]
    outs = ((af_ref, bf_ref), (ab_ref, bb_ref))
    for n in range(LRU_BLOCKS):
        sl = slice(n * LRU_BLOCK, (n + 1) * LRU_BLOCK)
        xb = xc[:, sl]
        xb16 = xb.astype(BF16)
        for dr in range(2):
            r = _sigmoid(jnp.dot(xb16, wa_ref[dr, n], preferred_element_type=F32) + ba_ref[dr:dr + 1, sl])
            g = _sigmoid(jnp.dot(xb16, wi_ref[dr, n], preferred_element_type=F32) + bi_ref[dr:dr + 1, sl])
            z = -lam_ref[dr:dr + 1, sl]
            softplus = jnp.maximum(z, 0.0) + jnp.log(1.0 + jnp.exp(-jnp.abs(z)))
            log_a = -LRU_C * r * softplus
            a_out, b_out = outs[dr]
            a_out[:, sl] = jnp.exp(log_a)
            b_out[:, sl] = jnp.sqrt(-_expm1(2.0 * log_a)) * g * xb


def lru_gates(proj, conv_w, conv_b, w_a, b_a, w_i, b_i, lam, *, n_prompt, t_p, t_s, tt=256):
    n = proj.shape[0]
    xcol = (3 * A_WIDTH) // LRU_WIDTH
    hb = tt // LRU_HALO
    nh = n // LRU_HALO
    full = lambda shape: pl.BlockSpec(shape, lambda i: (0,) * len(shape))
    out = jax.ShapeDtypeStruct((n, LRU_WIDTH), F32)
    ospec = pl.BlockSpec((tt, LRU_WIDTH), lambda i: (i, 0))
    return pl.pallas_call(
        functools.partial(_lru_gates_kernel, tt=tt, n_prompt=n_prompt, t_p=t_p, t_s=t_s),
        grid=(n // tt,),
        in_specs=[pl.BlockSpec((tt, LRU_WIDTH), lambda i: (i, xcol)),
                  pl.BlockSpec((LRU_HALO, LRU_WIDTH), lambda i: (jnp.maximum(i * hb - 1, 0), xcol)),
                  pl.BlockSpec((LRU_HALO, LRU_WIDTH), lambda i: (jnp.minimum((i + 1) * hb, nh - 1), xcol)),
                  full((CONV_W, LRU_WIDTH)), full((1, LRU_WIDTH)),
                  full((2, LRU_BLOCKS, LRU_BLOCK, LRU_BLOCK)), full((2, LRU_WIDTH)),
                  full((2, LRU_BLOCKS, LRU_BLOCK, LRU_BLOCK)), full((2, LRU_WIDTH)),
                  full((2, LRU_WIDTH))],
        out_specs=[ospec] * 4,
        out_shape=[out] * 4,
        compiler_params=_cparams(("arbitrary",)),
        name="lru_gates",
    )(proj, proj, proj, conv_w, conv_b.reshape(1, LRU_WIDTH), w_a.astype(BF16), b_a,
      w_i.astype(BF16), b_i, lam)


def _lru_scan_kernel(*refs, tt, reverse, combine):
    if combine:
        a_ref, b_ref, h0_ref, hf_ref, g_ref, o_ref, fin_ref, h_scr = refs
    else:
        a_ref, b_ref, h0_ref, o_ref, fin_ref, h_scr = refs
    ti = pl.program_id(1)

    @pl.when(ti == 0)
    def _():
        h_scr[...] = h0_ref[...]

    def step(j, h):
        t = tt - 1 - j if reverse else j
        h = a_ref[t] * h + b_ref[t]
        if combine:
            o_ref[t] = ((hf_ref[t] + h) * _gelu(g_ref[t])).astype(o_ref.dtype)
        else:
            o_ref[t] = h
        return h

    h = lax.fori_loop(0, tt, step, h_scr[...], unroll=8)
    h_scr[...] = h
    fin_ref[...] = h


def lru_scan(a, b, h0, tok0, nb, t, *, reverse, hf=None, proj=None, tt=256):
    tt = min(tt, t)
    nt = t // tt
    r0 = tok0 // tt
    combine = hf is not None
    vreg = (SUBLANES, LANES)
    view = lambda z: z.reshape(z.shape[0], LRU_WIDTH // LANES, LANES)

    def tmap(bi, ti):
        return nt - 1 - ti if reverse else ti

    in_specs = [pl.BlockSpec((tt,) + vreg, lambda bi, ti: (r0 + bi * nt + tmap(bi, ti), 0, 0)),
                pl.BlockSpec((tt,) + vreg, lambda bi, ti: (r0 + bi * nt + tmap(bi, ti), 0, 0)),
                pl.BlockSpec((None,) + vreg, lambda bi, ti: (bi, 0, 0))]
    args = [view(a), view(b), view(h0)]
    if combine:
        gcol = (3 * A_WIDTH + LRU_WIDTH) // LRU_WIDTH
        in_specs += [pl.BlockSpec((tt,) + vreg, lambda bi, ti: (bi * nt + tmap(bi, ti), 0, 0)),
                     pl.BlockSpec((tt,) + vreg, lambda bi, ti: (r0 + bi * nt + tmap(bi, ti), gcol, 0))]
        args += [view(hf), proj.reshape(proj.shape[0], EVEN_IN // LANES, LANES)]
    out, fin = pl.pallas_call(
        functools.partial(_lru_scan_kernel, tt=tt, reverse=reverse, combine=combine),
        grid=(nb, nt),
        in_specs=in_specs,
        out_specs=[pl.BlockSpec((tt,) + vreg, lambda bi, ti: (bi * nt + tmap(bi, ti), 0, 0)),
                   pl.BlockSpec((None,) + vreg, lambda bi, ti: (bi, 0, 0))],
        out_shape=[jax.ShapeDtypeStruct((nb * t,) + vreg, F32),
                   jax.ShapeDtypeStruct((nb,) + vreg, F32)],
        scratch_shapes=[pltpu.VMEM(vreg, F32)],
        compiler_params=_cparams(("arbitrary", "arbitrary")),
        name="lru_scan_bwd" if reverse else "lru_scan_fwd",
    )(*args)
    return out.reshape(nb * t, LRU_WIDTH), fin.reshape(nb, LRU_WIDTH)


def _topk_rows(s_list, k, payloads=None):
    n = s_list[0].shape[0]
    iota = lax.broadcasted_iota(jnp.int32, s_list[0].shape, 0).astype(F32)
    s_list = list(s_list)
    vals = [[] for _ in s_list]
    idxs = [[] for _ in s_list]
    for _ in range(k):
        for j, s in enumerate(s_list):
            m = jnp.max(s, axis=0, keepdims=True)
            am = jnp.min(jnp.where(s == m, iota, float(n)), axis=0, keepdims=True)
            sel = iota == am
            vals[j].append(m)
            if payloads is None:
                idxs[j].append(am)
            else:
                idxs[j].append(jnp.sum(jnp.where(sel, payloads[j], 0.0), axis=0, keepdims=True))
            s_list[j] = jnp.where(sel, -jnp.inf, s)
    return [(jnp.concatenate(v, axis=0), jnp.concatenate(i, axis=0)) for v, i in zip(vals, idxs)]


def _topk_rows_paired(s_list, k):
    n = s_list[0].shape[0]
    half = n // 2
    iota = lax.broadcasted_iota(jnp.int32, (half,) + s_list[0].shape[1:], 0).astype(F32)
    iota_b = iota + float(half)
    state = []
    for s in s_list:
        a, b = s[:half], s[half:]
        ge = a >= b
        state.append((jnp.where(ge, a, b), jnp.where(ge, b, a),
                      jnp.where(ge, iota, iota_b), jnp.where(ge, iota_b, iota)))
    vals = [[] for _ in s_list]
    idxs = [[] for _ in s_list]
    for _ in range(k):
        for j, (hi, lo, ih, il) in enumerate(state):
            m = jnp.max(hi, axis=0, keepdims=True)
            am = jnp.min(jnp.where(hi == m, ih, float(n)), axis=0, keepdims=True)
            sel = ih == am
            vals[j].append(m)
            idxs[j].append(am)
            state[j] = (jnp.where(sel, lo, hi), jnp.where(sel, -jnp.inf, lo),
                        jnp.where(sel, il, ih), il)
    return [(jnp.concatenate(v, axis=0), jnp.concatenate(i, axis=0)) for v, i in zip(vals, idxs)]


def _peer_route_kernel(q_ref, sk_ref, idx_ref, gate_ref, *, tm):
    half = PEER_QDIM // 2
    n_chunk = tm // LANES
    scores = []
    for c in range(n_chunk):
        q = q_ref[c * LANES:(c + 1) * LANES, :].astype(BF16)
        for p in range(2):
            scores.append(lax.dot_general(sk_ref[p], q[:, p * half:(p + 1) * half], NT_DIMS,
                                          preferred_element_type=F32))
    tops = _topk_rows_paired(scores, PEER_TOPK)
    nb = [PEER_TOPK // (a + 1) for a in range(PEER_TOPK)]
    pad = -sum(nb) % SUBLANES
    cands, cidxs = [], []
    for c in range(n_chunk):
        (s1, i1), (s2, i2) = tops[2 * c], tops[2 * c + 1]
        cands.append(jnp.concatenate([s1[a:a + 1] + s2[:nb[a]] for a in range(PEER_TOPK)]
                                     + [jnp.full((pad, LANES), -jnp.inf, F32)], axis=0))
        cidxs.append(jnp.concatenate(
            [i1[a:a + 1] * float(PEER_NKEYS) + i2[:nb[a]] for a in range(PEER_TOPK)]
            + [jnp.zeros((pad, LANES), F32)], axis=0))
    best = _topk_rows(cands, PEER_TOPK, payloads=cidxs)
    for c, (bs, bidx) in enumerate(best):
        e = jnp.exp(bs - bs[0:1])
        gates = e / jnp.sum(e, axis=0, keepdims=True)
        idx_ref[:, c * LANES:(c + 1) * LANES] = bidx.astype(jnp.int32)
        gate_ref[:, c * LANES:(c + 1) * LANES] = gates


def peer_route(q, sub_keys, tm=512):
    n = q.shape[0]
    half = PEER_QDIM // 2
    return pl.pallas_call(
        functools.partial(_peer_route_kernel, tm=tm),
        grid=(n // tm, PEER_HEADS),
        in_specs=[pl.BlockSpec((tm, PEER_QDIM), lambda i, h: (i, h)),
                  pl.BlockSpec((None, 2, PEER_NKEYS, half), lambda i, h: (h, 0, 0, 0))],
        out_specs=[pl.BlockSpec((None, PEER_TOPK, tm), lambda i, h: (h, 0, i))] * 2,
        out_shape=[jax.ShapeDtypeStruct((PEER_HEADS, PEER_TOPK, n), jnp.int32),
                   jax.ShapeDtypeStruct((PEER_HEADS, PEER_TOPK, n), F32)],
        compiler_params=_cparams(("arbitrary", "arbitrary")),
        name="peer_route",
    )(q, sub_keys)


def _peer_gates_kernel(idx_ref, gate_ref, g_ref):
    idx = idx_ref[...]
    gate = gate_ref[...]
    tg, _, hk = idx.shape
    i1 = (idx // PEER_NKEYS).astype(F32).astype(BF16)
    i2 = (idx % PEER_NKEYS).astype(F32).astype(BF16)
    iota = lax.broadcasted_iota(jnp.int32, (1, PEER_NKEYS, hk), 1).astype(F32).astype(BF16)
    zero = jnp.zeros((), BF16)
    a = jnp.where(i1 == iota, gate.astype(BF16), zero)
    bsel = jnp.where(i2 == iota, jnp.ones((), BF16), zero)
    g3 = lax.dot_general(a, bsel, (((2,), (2,)), ((0,), (0,))), preferred_element_type=F32)
    g_ref[...] = pltpu.einshape("nij->inj", g3).astype(g_ref.dtype)


def peer_gate_rows(idx, gates, tg=64):
    n, hk = idx.shape
    return pl.pallas_call(
        _peer_gates_kernel,
        grid=(n // tg,),
        in_specs=[pl.BlockSpec((tg, 1, hk), lambda i: (i, 0, 0))] * 2,
        out_specs=pl.BlockSpec((PEER_NKEYS, tg, PEER_NKEYS), lambda i: (0, i, 0)),
        out_shape=jax.ShapeDtypeStruct((PEER_NKEYS, n, PEER_NKEYS), BF16),
        compiler_params=_cparams(("arbitrary",)),
        name="peer_gate_rows",
    )(idx.reshape(n, 1, hk), gates.reshape(n, 1, hk))


def _peer_dense_kernel(h_ref, ut_ref, v_ref, gr_ref, x_ref, mg_ref, fg_ref, o_ref, *, final_norm,
                       row_split):
    e = pl.program_id(1)

    @pl.when(e == 0)
    def _():
        o_ref[...] = jnp.zeros_like(o_ref)

    tr = h_ref.shape[0] // row_split
    scores = [jnp.dot(h_ref[r * tr:(r + 1) * tr, :], ut_ref[...], preferred_element_type=F32)
              for r in range(row_split)]
    for r, s in enumerate(scores):
        rows = slice(r * tr, (r + 1) * tr)
        act = jnp.concatenate(
            [(_gelu(s[:, j * PEER_NKEYS:(j + 1) * PEER_NKEYS])
              * gr_ref[j, rows, :].astype(F32)).astype(BF16)
             for j in range(gr_ref.shape[0])], axis=1)
        o_ref[rows, :] += jnp.dot(act, v_ref[...], preferred_element_type=F32)

    @pl.when(e == pl.num_programs(1) - 1)
    def _():
        y = x_ref[...] + mg_ref[...] * o_ref[...]
        if final_norm:
            y = y * lax.rsqrt(jnp.mean(y * y, axis=-1, keepdims=True) + EPS) * fg_ref[...]
        o_ref[...] = y


def peer_dense(h, ut16, v16, grows, x, mod, gate_idx, final_g, *, n_prompt, t_s, final_norm,
               tm=1024, te=1024, row_split=2):
    n, d = x.shape
    ne = v16.shape[0]
    row = functools.partial(_mod_row, tm=tm, n_prompt=n_prompt, t_s=t_s)
    return pl.pallas_call(
        functools.partial(_peer_dense_kernel, final_norm=final_norm, row_split=row_split),
        grid=(n // tm, ne // te),
        in_specs=[pl.BlockSpec((tm, d), lambda i, e: (i, 0)),
                  pl.BlockSpec((d, te), lambda i, e: (0, e)),
                  pl.BlockSpec((te, d), lambda i, e: (e, 0)),
                  pl.BlockSpec((te // PEER_NKEYS, tm, PEER_NKEYS), lambda i, e: (e, i, 0)),
                  pl.BlockSpec((tm, d), lambda i, e: (i, 0), pipeline_mode=pl.Buffered(1)),
                  pl.BlockSpec((None, None, 1, d), lambda i, e: (row(i), gate_idx, 0, 0)),
                  pl.BlockSpec((1, d), lambda i, e: (0, 0))],
        out_specs=pl.BlockSpec((tm, d), lambda i, e: (i, 0), pipeline_mode=pl.Buffered(1)),
        out_shape=jax.ShapeDtypeStruct((n, d), F32),
        compiler_params=_cparams(("arbitrary", "arbitrary")),
        name="peer_dense",
    )(h, ut16, v16, grows, x, mod, final_g.reshape(1, d))


def peer_block(x, norm_g, mod, w_q16, sk16, ut16, v16, final_g, *, n_prompt, t_s, final_norm):
    n = x.shape[0]
    q, h = norm_matmul(x, norm_g, mod, 3, 4, w_q16, n_prompt=n_prompt, t_s=t_s, emit_h=True)
    idx, gates = peer_route(q, sk16)
    hk = PEER_HEADS * PEER_TOPK
    idx = idx.reshape(hk, n).T
    gates = gates.reshape(hk, n).T
    grows = peer_gate_rows(idx, gates)
    return peer_dense(h, ut16, v16, grows, x, mod, 5, final_g, n_prompt=n_prompt, t_s=t_s,
                      final_norm=final_norm)


def kernel(x_prompt, x_sample, cache_a_k, cache_a_v, state_lru, cache_c_k, cache_c_v, c, c_ctx, w_mod, b_mod, norm1_g, norm2_g, final_norm_g, even_w_in, even_w_out, a_lambda, a_subln_g, lru_conv_w, lru_conv_b, lru_w_a, lru_b_a, lru_w_i, lru_b_i, lru_lambda, odd_w_in, odd_w_out, na_rpb, peer_w_q, peer_sub_keys, peer_u, peer_v):
    b_p, t_p, d = x_prompt.shape
    b_s, t_s, _ = x_sample.shape
    depth = w_mod.shape[0]
    n_p = b_p * t_p
    n_s = b_s * t_s
    lc = cache_a_k.shape[2]
    tok = dict(n_prompt=n_p, t_s=t_s)

    x = jnp.concatenate([x_prompt.reshape(n_p, d), x_sample.reshape(n_s, d)], axis=0)
    n_rows = -(-(1 + b_s) // SUBLANES) * SUBLANES
    cond = jnp.concatenate([c_ctx[None], c, jnp.zeros((n_rows - 1 - b_s, d), F32)], axis=0)
    mod_all = modulation_all(cond, w_mod, b_mod).reshape(depth, n_rows, 6, 1, d)
    rope = rope_tables(t_s)

    new_ak, new_av, new_lru, new_ck, new_cv = [], [], [], [], []
    for l in range(depth):
        mod = mod_all[l]
        if l % 2 == 0:
            e = l // 2
            lam_init = 0.8 - 0.6 * math.exp(-0.3 * l)
            proj = norm_matmul(x, norm1_g[l], mod, 0, 1, even_w_in[e].astype(BF16), **tok)
            ctx = (cache_a_k[:, e].reshape(b_s, lc, A_WIDTH), cache_a_v[:, e].reshape(b_s, lc, A_WIDTH))
            oa_p = diff_attention_call(proj, 0, b_p, t_p, a_lambda[e], a_subln_g[e], lam_init)
            oa_s = diff_attention_call(proj, n_p, b_s, t_s, a_lambda[e], a_subln_g[e], lam_init,
                                       ctx=ctx, rope=rope)
            a_f, b_f, a_b, b_b = lru_gates(proj, lru_conv_w[e], lru_conv_b[e], lru_w_a[e], lru_b_a[e],
                                           lru_w_i[e], lru_b_i[e], lru_lambda[e],
                                           n_prompt=n_p, t_p=t_p, t_s=t_s)
            zeros_p = jnp.zeros((b_p, LRU_WIDTH), F32)
            hf_p, fin_f = lru_scan(a_f, b_f, zeros_p, 0, b_p, t_p, reverse=False)
            ob_p, fin_b = lru_scan(a_b, b_b, zeros_p, 0, b_p, t_p, reverse=True, hf=hf_p, proj=proj)
            hf_s, _ = lru_scan(a_f, b_f, state_lru[:, e, 0], n_p, b_s, t_s, reverse=False)
            ob_s, _ = lru_scan(a_b, b_b, state_lru[:, e, 1], n_p, b_s, t_s, reverse=True,
                               hf=hf_s, proj=proj)
            w_out = even_w_out[e].astype(BF16)
            x = matmul_res([(oa_p, oa_s), (ob_p, ob_s)], [w_out[:A_WIDTH], w_out[A_WIDTH:]],
                           x, mod, 2, **tok)
            new_ak.append(proj[:n_p, A_WIDTH:2 * A_WIDTH].reshape(b_p, t_p, A_HEADS, 2 * A_QK_DIM))
            new_av.append(proj[:n_p, 2 * A_WIDTH:3 * A_WIDTH].reshape(b_p, t_p, A_HEADS, A_V_DIM))
            new_lru.append(jnp.stack([fin_f, fin_b], axis=1))
        else:
            o = l // 2
            proj = norm_matmul(x, norm1_g[l], mod, 0, 1, odd_w_in[o].astype(BF16), **tok)
            oc_p = attention_call(proj, b_p, t_p)
            bias = na_bias_tables(na_rpb[o], t_s // GRID_W)
            oc_s = na_call(proj, n_p, b_s, t_s, cache_c_k[:, o].reshape(b_s, lc, C_WIDTH),
                           cache_c_v[:, o].reshape(b_s, lc, C_WIDTH), bias)
            x = matmul_res([(oc_p, oc_s)], [odd_w_out[o].astype(BF16)], x, mod, 2, **tok)
            new_ck.append(proj[:n_p, C_WIDTH:2 * C_WIDTH].reshape(b_p, t_p, C_HEADS, C_HEAD_DIM))
            new_cv.append(proj[:n_p, 2 * C_WIDTH:3 * C_WIDTH].reshape(b_p, t_p, C_HEADS, C_HEAD_DIM))
        x = peer_block(x, norm2_g[l], mod, peer_w_q[l].astype(BF16), peer_sub_keys[l].astype(BF16),
                       peer_u[l].astype(BF16).T, peer_v[l].astype(BF16), final_norm_g,
                       final_norm=(l == depth - 1), **tok)
    y_prompt = x[:n_p].reshape(b_p, t_p, d)
    y_sample = x[n_p:].reshape(b_s, t_s, d)
    return (y_prompt, y_sample, jnp.stack(new_ak, axis=1), jnp.stack(new_av, axis=1),
            jnp.stack(new_lru, axis=1), jnp.stack(new_ck, axis=1), jnp.stack(new_cv, axis=1))
```
